```python
import math
import jax, jax.numpy as jnp
from jax import lax
import numpy as np

D_MODEL = 1024
BATCH = 8
SEQ = 2048
DEPTH = 4
DEC_BATCH = 128
DEC_SEQ = 4
PAST_LEN = 16384
PAGE_SIZE = 128

BRANCH_WIDTH = D_MODEL // 2
N_BRANCH = 3
S5_GROUP_CH = 16
S5_GROUPS = BRANCH_WIDTH // S5_GROUP_CH
S5_STATE = 64
HG_KDIM = 128
HG_HEADS = BRANCH_WIDTH // HG_KDIM
HG_VDIM = BRANCH_WIDTH // HG_HEADS
HG_CHUNK = 64
RW_HEAD = 64
RW_HEADS = BRANCH_WIDTH // RW_HEAD
RW_DECAY_LORA = 64
RW_A_LORA = 64
RW_GATE_LORA = 128
RW_COLS = 3 * BRANCH_WIDTH + RW_DECAY_LORA + RW_A_LORA + RW_GATE_LORA
RW_LN_EPS = 64e-5
IN_SPLITS = [BRANCH_WIDTH, 2 * BRANCH_WIDTH, 3 * BRANCH_WIDTH, 4 * BRANCH_WIDTH, 5 * BRANCH_WIDTH,
             5 * BRANCH_WIDTH + RW_COLS]
IN_COLS = 5 * BRANCH_WIDTH + RW_COLS + N_BRANCH * D_MODEL
RW_SPLITS = [BRANCH_WIDTH, 2 * BRANCH_WIDTH, 3 * BRANCH_WIDTH, 3 * BRANCH_WIDTH + RW_DECAY_LORA,
             3 * BRANCH_WIDTH + RW_DECAY_LORA + RW_A_LORA]
N_EXPERTS = 32
TOP_K = 4
D_FF = D_MODEL
SWIGLU_LIMIT = 7.0
SWIGLU_ALPHA = 1.702
NORM_EPS = 1e-6

kernel_name = 'hybrid_s5_hgrn2_rwkv7_moe_step'


def _rmsnorm(x, g):
    xf = x.astype(jnp.float32)
    y = xf * lax.rsqrt(jnp.mean(xf * xf, axis=-1, keepdims=True) + NORM_EPS)
    return (y * g.astype(jnp.float32)).astype(x.dtype)


def _cplx_affine_combine(e1, e2):
    a1r, a1i, b1r, b1i = e1
    a2r, a2i, b2r, b2i = e2
    return (a2r * a1r - a2i * a1i, a2r * a1i + a2i * a1r,
            a2r * b1r - a2i * b1i + b2r, a2r * b1i + a2i * b1r + b2i)


def _s5_branch(u, lam_re, lam_im, log_step, b_re, b_im, c_re, c_im, d_skip, w_glu, h0_re, h0_im):
    f32 = jnp.float32
    bsz, t_len, _ = u.shape
    uf = u.astype(f32).reshape(bsz, t_len, S5_GROUPS, S5_GROUP_CH)
    lr = lam_re.astype(f32)
    li = lam_im.astype(f32)
    dt = jnp.exp(log_step.astype(f32))[:, None]
    mag = jnp.exp(lr * dt)
    ab_re = mag * jnp.cos(li * dt)
    ab_im = mag * jnp.sin(li * dt)
    den = lr * lr + li * li
    zr = ((ab_re - 1.0) * lr + ab_im * li) / den
    zi = (ab_im * lr - (ab_re - 1.0) * li) / den
    br = b_re.astype(f32)
    bi = b_im.astype(f32)
    bb_re = zr[..., None] * br - zi[..., None] * bi
    bb_im = zr[..., None] * bi + zi[..., None] * br
    bu_re = jnp.einsum('gpc,btgc->btgp', bb_re, uf)
    bu_im = jnp.einsum('gpc,btgc->btgp', bb_im, uf)
    h0r = h0_re.astype(f32)
    h0i = h0_im.astype(f32)
    bu_re = bu_re.at[:, 0].add(ab_re * h0r - ab_im * h0i)
    bu_im = bu_im.at[:, 0].add(ab_re * h0i + ab_im * h0r)
    a_re = jnp.broadcast_to(ab_re, bu_re.shape)
    a_im = jnp.broadcast_to(ab_im, bu_im.shape)
    _, _, hr, hi = lax.associative_scan(_cplx_affine_combine, (a_re, a_im, bu_re, bu_im), axis=1)
    y = (jnp.einsum('gcp,btgp->btgc', c_re.astype(f32), hr)
         - jnp.einsum('gcp,btgp->btgc', c_im.astype(f32), hi)
         + d_skip.astype(f32).reshape(S5_GROUPS, S5_GROUP_CH) * uf)
    y = y.reshape(bsz, t_len, BRANCH_WIDTH)
    g = jax.nn.gelu(y)
    out = g * jax.nn.sigmoid(g @ w_glu.astype(f32))
    return out.astype(u.dtype), hr[:, -1], hi[:, -1]


def _hgrn2_chunked(q, k, log_f, v, s0):
    bsz, t_len, nh, _ = q.shape
    csz = HG_CHUNK if t_len % HG_CHUNK == 0 else t_len
    n_chunks = t_len // csz

    def to_chunks(a):
        return a.reshape(bsz, n_chunks, csz, nh, a.shape[-1]).transpose(1, 0, 3, 2, 4)

    mask = jnp.tril(jnp.ones((csz, csz), dtype=bool))[:, :, None]

    def step(state, inp):
        qc, kc, lfc, vc = inp
        b = jnp.cumsum(lfc, axis=2)
        diff = b[:, :, :, None, :] - b[:, :, None, :, :]
        decay = jnp.exp(jnp.where(mask, diff, -jnp.inf))
        scores = jnp.einsum('bhtk,bhsk,bhtsk->bhts', qc, kc, decay)
        o = (jnp.einsum('bhts,bhsv->bhtv', scores, vc)
             + jnp.einsum('bhtk,bhkv->bhtv', qc * jnp.exp(b), state))
        b_last = b[:, :, -1:, :]
        state = (jnp.exp(b_last[:, :, 0, :])[..., None] * state
                 + jnp.einsum('bhsk,bhsv->bhkv', kc * jnp.exp(b_last - b), vc))
        return state, o

    s_final, o = lax.scan(step, s0, (to_chunks(q), to_chunks(k), to_chunks(log_f), to_chunks(v)))
    o = o.transpose(1, 0, 3, 2, 4).reshape(bsz, t_len, nh, v.shape[-1])
    return o, s_final


def _hgrn2_branch(q_raw, f_raw, i_raw, g_raw, lb, gnorm_w, s0):
    f32 = jnp.float32
    bsz, t_len, _ = q_raw.shape
    hs = (bsz, t_len, HG_HEADS, HG_KDIM)
    q = jax.nn.silu(q_raw.astype(f32)).reshape(hs)
    z = f_raw.astype(f32).reshape(hs)
    lbh = lb.reshape(HG_HEADS, HG_KDIM)
    log_f = jnp.log(lbh + (1.0 - lbh) * jax.nn.sigmoid(z))
    k = (1.0 - lbh) * jax.nn.sigmoid(-z)
    v = i_raw.astype(f32).reshape(bsz, t_len, HG_HEADS, HG_VDIM)
    o, s_new = _hgrn2_chunked(q, k, log_f, v, s0.astype(f32))
    o = o * lax.rsqrt(jnp.mean(o * o, axis=-1, keepdims=True) + NORM_EPS) * gnorm_w.astype(f32)
    o = o * jax.nn.silu(g_raw.astype(f32).reshape(bsz, t_len, HG_HEADS, HG_VDIM))
    return o.reshape(bsz, t_len, BRANCH_WIDTH).astype(q_raw.dtype), s_new


def _rwkv7_scan(r, w, k, v, a_vec, b_vec, s0):
    def step(state, inp):
        r_t, w_t, k_t, v_t, a_t, b_t = inp
        sa = jnp.einsum('bhij,bhj->bhi', state, a_t)
        state = (state * w_t[:, :, None, :] + sa[..., None] * b_t[:, :, None, :]
                 + v_t[..., None] * k_t[:, :, None, :])
        y = jnp.einsum('bhij,bhj->bhi', state, r_t)
        return state, y

    xs = tuple(a.transpose(1, 0, 2, 3) for a in (r, w, k, v, a_vec, b_vec))
    s_final, ys = lax.scan(step, s0, xs)
    return ys.transpose(1, 0, 2, 3), s_final


def _rwkv7_branch(p, prev, mu, w0, w2, a0, a2, g2, k_k, k_a, r_k, ln_w, ln_b, s0):
    f32 = jnp.float32
    bsz, t_len, _ = p.shape
    pf = p.astype(f32)
    p_prev = jnp.concatenate([prev.astype(f32)[:, None, :], pf[:, :-1]], axis=1)
    m = pf + (p_prev - pf) * mu.astype(f32)
    r, k, v, dw, da, dg = jnp.split(m, RW_SPLITS, axis=-1)
    w_log = -jax.nn.softplus(-(w0.astype(f32) + jnp.tanh(dw) @ w2.astype(f32))) - 0.5
    decay = jnp.exp(-jnp.exp(w_log))
    a = jax.nn.sigmoid(a0.astype(f32) + da @ a2.astype(f32))
    g = jax.nn.sigmoid(dg) @ g2.astype(f32)
    hs = (bsz, t_len, RW_HEADS, RW_HEAD)
    kk = (k * k_k.astype(f32)).reshape(hs)
    kk = kk / jnp.maximum(jnp.sqrt(jnp.sum(kk * kk, axis=-1, keepdims=True)), 1e-12)
    k = k * (1.0 + (a - 1.0) * k_a.astype(f32))
    rh, kh, vh = r.reshape(hs), k.reshape(hs), v.reshape(hs)
    y, s_new = _rwkv7_scan(rh, decay.reshape(hs), kh, vh, -kk, kk * a.reshape(hs), s0.astype(f32))
    mean = jnp.mean(y, axis=-1, keepdims=True)
    var = jnp.mean((y - mean) ** 2, axis=-1, keepdims=True)
    y = ((y - mean) * lax.rsqrt(var + RW_LN_EPS) * ln_w.astype(f32).reshape(RW_HEADS, RW_HEAD)
         + ln_b.astype(f32).reshape(RW_HEADS, RW_HEAD))
    y = y + jnp.sum(rh * kh * r_k.astype(f32), axis=-1, keepdims=True) * vh
    out = y.reshape(bsz, t_len, BRANCH_WIDTH) * g
    return out.astype(p.dtype), s_new, p[:, -1]


def _moe(h, w_r, b_r, w_gu, b_gu, w_d, b_d):
    lead = h.shape[:-1]
    t = h.reshape(-1, D_MODEL)
    logits = (t @ w_r + b_r).astype(jnp.float32)
    top_val, top_idx = lax.top_k(logits, TOP_K)
    top_w = jax.nn.softmax(top_val, axis=-1)
    combine = jnp.einsum('nk,nke->ne', top_w,
                         jax.nn.one_hot(top_idx, N_EXPERTS, dtype=jnp.float32)).astype(h.dtype)
    out = jnp.zeros_like(t)
    for e in range(N_EXPERTS):
        gu = t @ w_gu[e] + b_gu[e]
        gate = jnp.minimum(gu[:, :D_FF], SWIGLU_LIMIT)
        up = jnp.clip(gu[:, D_FF:], -SWIGLU_LIMIT, SWIGLU_LIMIT)
        act = (up + 1.0) * gate * jax.nn.sigmoid(SWIGLU_ALPHA * gate)
        out = out + combine[:, e:e + 1] * (act @ w_d[e] + b_d[e])
    return out.reshape(lead + (D_MODEL,))


def _trunk(x, c, s5r0, s5i0, hg0, rw0, sh0, prm):
    f32 = jnp.float32
    lb_soft = jax.nn.softmax(prm['hg_lb'].astype(f32), axis=0)
    lb_all = jnp.cumsum(lb_soft, axis=0) - lb_soft[0:1]
    new_s5r, new_s5i, new_hg, new_rw, new_sh = [], [], [], [], []
    for l in range(DEPTH):
        mod = jax.nn.silu(c) @ prm['w_ada'][l] + prm['b_ada'][l]
        shift1, scale1, gate1, shift2, scale2, gate2 = jnp.split(mod[:, None, :], 6, axis=-1)
        h = _rmsnorm(x, prm['g_norm1'][l]) * (1.0 + scale1) + shift1
        proj = h @ prm['w_in'][l]
        u_a, q_b, f_b, i_b, g_b, p_c, gate_cols = jnp.split(proj, IN_SPLITS, axis=-1)
        y_a, hr, hi = _s5_branch(u_a, prm['s5_lam_re'][l], prm['s5_lam_im'][l], prm['s5_log_step'][l],
                                 prm['s5_b_re'][l], prm['s5_b_im'][l], prm['s5_c_re'][l], prm['s5_c_im'][l],
                                 prm['s5_d'][l], prm['s5_w_glu'][l], s5r0[l], s5i0[l])
        y_b, s_hg = _hgrn2_branch(q_b, f_b, i_b, g_b, lb_all[l], prm['hg_gnorm'][l], hg0[l])
        y_c, s_rw, last = _rwkv7_branch(p_c, sh0[l], prm['rw_mu'][l], prm['rw_w0'][l], prm['rw_w2'][l],
                                        prm['rw_a0'][l], prm['rw_a2'][l], prm['rw_g2'][l],
                                        prm['rw_k_k'][l], prm['rw_k_a'][l], prm['rw_r_k'][l],
                                        prm['rw_ln_w'][l], prm['rw_ln_b'][l], rw0[l])
        branches = jnp.stack([y_a, y_b, y_c], axis=-2)
        branch_d = jnp.einsum('btnw,nwd->btnd', branches, prm['w_branch'][l])
        gates = jax.nn.sigmoid(gate_cols.reshape(gate_cols.shape[:-1] + (N_BRANCH, D_MODEL)))
        mixed = jnp.sum(gates * branch_d, axis=-2) @ prm['w_out'][l]
        x = x + gate1 * mixed
        h2 = _rmsnorm(x, prm['g_norm2'][l]) * (1.0 + scale2) + shift2
        x = x + gate2 * _moe(h2, prm['w_router'][l], prm['b_router'][l], prm['w_gate_up'][l],
                             prm['b_gate_up'][l], prm['w_down'][l], prm['b_down'][l])
        new_s5r.append(hr.astype(x.dtype))
        new_s5i.append(hi.astype(x.dtype))
        new_hg.append(s_hg.astype(x.dtype))
        new_rw.append(s_rw.astype(x.dtype))
        new_sh.append(last.astype(x.dtype))
    y = _rmsnorm(x, prm['g_final'])
    return (y, jnp.stack(new_s5r), jnp.stack(new_s5i), jnp.stack(new_hg), jnp.stack(new_rw),
            jnp.stack(new_sh))


def setup_inputs(seed: int = 0) -> dict:
    key = jax.random.key(seed)
    ks = iter(jax.random.split(key, 64))

    def nrm(shape, scale=1.0):
        return scale * jax.random.normal(next(ks), shape, jnp.float32)

    L, G, P, C, W = DEPTH, S5_GROUPS, S5_STATE, S5_GROUP_CH, BRANCH_WIDTH
    n_idx = jnp.arange(P, dtype=jnp.float32)
    return {
        'x_prompt': nrm((BATCH, SEQ, D_MODEL)),
        'x_sample': nrm((DEC_BATCH, DEC_SEQ, D_MODEL)),
        'state_s5_re': nrm((L, DEC_BATCH, G, P), 0.5),
        'state_s5_im': nrm((L, DEC_BATCH, G, P), 0.5),
        'state_hgrn': nrm((L, DEC_BATCH, HG_HEADS, HG_KDIM, HG_VDIM), 0.5),
        'state_rwkv': nrm((L, DEC_BATCH, RW_HEADS, RW_HEAD, RW_HEAD), 0.3),
        'state_rwkv_shift': nrm((L, DEC_BATCH, RW_COLS)),
        'c_prompt': nrm((BATCH, D_MODEL)),
        'c_sample': nrm((DEC_BATCH, D_MODEL)),
        'g_norm1': 1.0 + nrm((L, D_MODEL), 0.1),
        'g_norm2': 1.0 + nrm((L, D_MODEL), 0.1),
        'w_ada': nrm((L, D_MODEL, 6 * D_MODEL), 0.5 * D_MODEL ** -0.5),
        'b_ada': nrm((L, 6 * D_MODEL), 0.05),
        'w_in': nrm((L, D_MODEL, IN_COLS), D_MODEL ** -0.5),
        's5_lam_re': -0.5 + nrm((L, G, P), 0.01),
        's5_lam_im': jnp.broadcast_to(math.pi * n_idx, (L, G, P)) + nrm((L, G, P), 0.01),
        's5_log_step': jax.random.uniform(next(ks), (L, G), jnp.float32, math.log(1e-3), math.log(1e-1)),
        's5_b_re': nrm((L, G, P, C), (2 * C) ** -0.5),
        's5_b_im': nrm((L, G, P, C), (2 * C) ** -0.5),
        's5_c_re': nrm((L, G, C, P), (2 * P) ** -0.5),
        's5_c_im': nrm((L, G, C, P), (2 * P) ** -0.5),
        's5_d': nrm((L, W)),
        's5_w_glu': nrm((L, W, W), W ** -0.5),
        'hg_lb': nrm((L, W)),
        'hg_gnorm': 1.0 + nrm((L, HG_VDIM), 0.1),
        'rw_mu': jax.random.uniform(next(ks), (L, RW_COLS), jnp.float32),
        'rw_w0': -1.0 + nrm((L, W), 0.5),
        'rw_w2': nrm((L, RW_DECAY_LORA, W), 0.1),
        'rw_a0': nrm((L, W), 0.5),
        'rw_a2': nrm((L, RW_A_LORA, W), 0.1),
        'rw_g2': nrm((L, RW_GATE_LORA, W), RW_GATE_LORA ** -0.5),
        'rw_k_k': 0.85 + nrm((L, W), 0.05),
        'rw_k_a': 1.0 + nrm((L, W), 0.05),
        'rw_r_k': nrm((L, RW_HEADS, RW_HEAD), 0.1),
        'rw_ln_w': 1.0 + nrm((L, W), 0.1),
        'rw_ln_b': nrm((L, W), 0.02),
        'w_branch': nrm((L, N_BRANCH, W, D_MODEL), W ** -0.5),
        'w_out': nrm((L, D_MODEL, D_MODEL), D_MODEL ** -0.5),
        'w_router': nrm((L, D_MODEL, N_EXPERTS), D_MODEL ** -0.5),
        'b_router': nrm((L, N_EXPERTS), 0.01),
        'w_gate_up': nrm((L, N_EXPERTS, D_MODEL, 2 * D_FF), D_MODEL ** -0.5),
        'b_gate_up': nrm((L, N_EXPERTS, 2 * D_FF), 0.01),
        'w_down': nrm((L, N_EXPERTS, D_FF, D_MODEL), D_FF ** -0.5),
        'b_down': nrm((L, N_EXPERTS, D_MODEL), 0.01),
        'g_final': 1.0 + nrm((D_MODEL,), 0.1),
    }


def reference(x_prompt, x_sample, state_s5_re, state_s5_im, state_hgrn, state_rwkv, state_rwkv_shift,
              c_prompt, c_sample, g_norm1, g_norm2, w_ada, b_ada, w_in, s5_lam_re, s5_lam_im, s5_log_step,
              s5_b_re, s5_b_im, s5_c_re, s5_c_im, s5_d, s5_w_glu, hg_lb, hg_gnorm, rw_mu, rw_w0, rw_w2,
              rw_a0, rw_a2, rw_g2, rw_k_k, rw_k_a, rw_r_k, rw_ln_w, rw_ln_b, w_branch, w_out, w_router,
              b_router, w_gate_up, b_gate_up, w_down, b_down, g_final):
    prm = dict(g_norm1=g_norm1, g_norm2=g_norm2, w_ada=w_ada, b_ada=b_ada, w_in=w_in,
               s5_lam_re=s5_lam_re, s5_lam_im=s5_lam_im, s5_log_step=s5_log_step, s5_b_re=s5_b_re,
               s5_b_im=s5_b_im, s5_c_re=s5_c_re, s5_c_im=s5_c_im, s5_d=s5_d, s5_w_glu=s5_w_glu,
               hg_lb=hg_lb, hg_gnorm=hg_gnorm, rw_mu=rw_mu, rw_w0=rw_w0, rw_w2=rw_w2, rw_a0=rw_a0,
               rw_a2=rw_a2, rw_g2=rw_g2, rw_k_k=rw_k_k, rw_k_a=rw_k_a, rw_r_k=rw_r_k, rw_ln_w=rw_ln_w,
               rw_ln_b=rw_ln_b, w_branch=w_branch, w_out=w_out, w_router=w_router, b_router=b_router,
               w_gate_up=w_gate_up, b_gate_up=b_gate_up, w_down=w_down, b_down=b_down, g_final=g_final)
    bp = x_prompt.shape[0]
    dt = x_prompt.dtype
    z_s5 = jnp.zeros((DEPTH, bp, S5_GROUPS, S5_STATE), dt)
    z_hg = jnp.zeros((DEPTH, bp, HG_HEADS, HG_KDIM, HG_VDIM), dt)
    z_rw = jnp.zeros((DEPTH, bp, RW_HEADS, RW_HEAD, RW_HEAD), dt)
    z_sh = jnp.zeros((DEPTH, bp, RW_COLS), dt)
    y_prompt, p_s5r, p_s5i, p_hg, p_rw, p_sh = _trunk(x_prompt, c_prompt, z_s5, z_s5, z_hg, z_rw, z_sh, prm)
    y_sample, s_s5r, s_s5i, s_hg, s_rw, s_sh = _trunk(x_sample, c_sample, state_s5_re, state_s5_im,
                                                        state_hgrn, state_rwkv, state_rwkv_shift, prm)
    return (y_prompt, y_sample, p_s5r, p_s5i, p_hg, p_rw, p_sh, s_s5r, s_s5i, s_hg, s_rw, s_sh)
```

```python
import functools
import math

import jax
import jax.numpy as jnp
from jax import lax
from jax.experimental import pallas as pl
from jax.experimental.pallas import tpu as pltpu

F32 = jnp.float32
BF16 = jnp.bfloat16

V7X_SUBLANES = 8
V7X_LANES = 128
V7X_VMEM_LIMIT = 60 * 1024 * 1024

ROW_TILE = 256
CHUNK_ROWS = 512
SEQ_GROUP = V7X_SUBLANES
HG_SUB = 16
MOE_BLOCK = 256

S5_GROUP_CH = 16
S5_STATE = 64
HG_KDIM = 128
RW_HEAD = 64
RW_DECAY_LORA = 64
RW_A_LORA = 64
RW_GATE_LORA = 128
RW_LN_EPS = 64e-5
N_EXPERTS = 32
TOP_K = 4
SWIGLU_LIMIT = 7.0
SWIGLU_ALPHA = 1.702
NORM_EPS = 1e-6
NEG_BIG = -1e30


def _bf(x):
    return x.astype(BF16)


def _dot(a, b):
    return jnp.dot(a, b, preferred_element_type=F32)


def _split(x):
    hi = _bf(x)
    lo = _bf(x - hi.astype(F32))
    return hi, lo


def _dot_hilo(x, w_bf):
    hi, lo = _split(x)
    return _dot(hi, w_bf) + _dot(lo, w_bf)


def _dot3(x, w):
    xh, xl = _split(x)
    wh, wl = _split(w)
    return _dot(xh, wh) + _dot(xh, wl) + _dot(xl, wh)


def _sigmoid(x):
    return 1.0 / (1.0 + jnp.exp(-x))


def _silu(x):
    return x * _sigmoid(x)


def _softplus(x):
    return jnp.maximum(x, 0.0) + jnp.log(1.0 + jnp.exp(-jnp.abs(x)))


def _gelu_tanh(x):
    return 0.5 * x * (1.0 + jnp.tanh(math.sqrt(2.0 / math.pi) * (x + 0.044715 * (x * x * x))))


def _rms(x, g):
    return x * lax.rsqrt(jnp.mean(x * x, axis=-1, keepdims=True) + NORM_EPS) * g


def _params(sem):
    return pltpu.CompilerParams(dimension_semantics=sem, vmem_limit_bytes=V7X_VMEM_LIMIT)


def _const_spec(shape):
    nd = len(shape)
    return pl.BlockSpec(shape, lambda *_: (0,) * nd, pipeline_mode=pl.Buffered(1))


def _ada_kernel(c_ref, w_ref, b_ref, o_ref):
    c = c_ref[...]
    o_ref[...] = _dot3(_silu(c), w_ref[...]) + b_ref[...]


def _ada_all(c_all, w_ada, b_ada):
    n_layers, d, d6 = w_ada.shape
    n = c_all.shape[0]
    col = d6 // 4
    return pl.pallas_call(
        _ada_kernel,
        out_shape=jax.ShapeDtypeStruct((n_layers, n, d6), F32),
        grid=(n_layers, d6 // col),
        in_specs=[pl.BlockSpec((n, d), lambda l, j: (0, 0)),
                  pl.BlockSpec((None, d, col), lambda l, j: (l, 0, j)),
                  pl.BlockSpec((None, 1, col), lambda l, j: (l, 0, j))],
        out_specs=pl.BlockSpec((None, n, col), lambda l, j: (l, 0, j)),
        compiler_params=_params(("arbitrary", "arbitrary")),
        name="ada_mod",
    )(c_all, w_ada, b_ada.reshape(n_layers, 1, d6))


def _combine_moe(x1, g2, tw, ys):
    acc = tw[:, 0:1] * ys[0]
    for k in range(1, TOP_K):
        acc = acc + tw[:, k:k + 1] * ys[k]
    return x1 + g2 * acc


def _inproj_kernel(*refs, has_moe, splits):
    if has_moe:
        x1_ref, g2_ref, tw_ref = refs[:3]
        y_refs = refs[3:3 + TOP_K]
        refs = refs[3 + TOP_K:]
        x = _combine_moe(x1_ref[...], g2_ref[...], tw_ref[...], [r[...] for r in y_refs])
    else:
        x = refs[0][...]
        refs = refs[1:]
    sh_ref, sc_ref, g_ref, w_ref = refs[:4]
    outs = refs[4:]
    if has_moe:
        outs[0][...] = x
        outs = outs[1:]
    h = _rms(x, g_ref[...]) * (1.0 + sc_ref[...]) + sh_ref[...]
    hb = _bf(h)
    for o_ref, (lo, hi) in zip(outs, splits):
        o_ref[...] = _dot(hb, w_ref[:, lo:hi])


def _mod_spec(layer, piece, d, n_prompt_tiles):
    return pl.BlockSpec((None, None, ROW_TILE, d), lambda i: (layer, i // n_prompt_tiles, 0, piece))


def _inproj(x_in, moe_in, modt, g_norm1, w_in_bf, layer, n_prompt_tiles, splits):
    n, d = x_in.shape
    n_tiles = n // ROW_TILE
    row = lambda c: pl.BlockSpec((ROW_TILE, c), lambda i: (i, 0))
    in_specs, args = [row(d)], [x_in]
    has_moe = moe_in is not None
    if has_moe:
        topw, yt = moe_in
        in_specs += [_mod_spec(layer - 1, 5, d, n_prompt_tiles), row(V7X_LANES)]
        args += [modt, topw]
        for k in range(TOP_K):
            in_specs.append(pl.BlockSpec((ROW_TILE, d), functools.partial(lambda i, k: (k * n_tiles + i, 0), k=k)))
            args.append(yt)
    in_specs += [_mod_spec(layer, 0, d, n_prompt_tiles), _mod_spec(layer, 1, d, n_prompt_tiles),
                 pl.BlockSpec((None, 1, d), lambda i: (layer, 0, 0)),
                 pl.BlockSpec((None,) + w_in_bf.shape[1:], lambda i: (layer, 0, 0), pipeline_mode=pl.Buffered(1))]
    args += [modt, modt, g_norm1, w_in_bf]
    widths = [hi - lo for lo, hi in splits]
    out_shape = [jax.ShapeDtypeStruct((n, c), F32) for c in widths]
    out_specs = [row(c) for c in widths]
    if has_moe:
        out_shape = [jax.ShapeDtypeStruct((n, d), F32)] + out_shape
        out_specs = [row(d)] + out_specs
    outs = pl.pallas_call(
        functools.partial(_inproj_kernel, has_moe=has_moe, splits=splits),
        out_shape=out_shape, grid=(n_tiles,), in_specs=in_specs, out_specs=out_specs,
        compiler_params=_params(("arbitrary",)), name="norm_inproj",
    )(*args)
    if has_moe:
        return outs[0], outs[1:]
    return x_in, outs


class _Region:
    def __init__(self, row_start, n_steps, n_seq):
        self.row_start, self.n_steps, self.n_seq = row_start, n_steps, n_seq
        assert n_seq % SEQ_GROUP == 0
        if n_seq == SEQ_GROUP:
            self.tc = min(n_steps, CHUNK_ROWS // SEQ_GROUP)
            assert n_steps % self.tc == 0
            self.block_rows = self.tc * SEQ_GROUP
            self.n_chunks = n_steps // self.tc
        else:
            self.tc = n_steps
            self.block_rows = n_steps * n_seq
            self.n_chunks = 1
        assert row_start % self.block_rows == 0
        self.block0 = row_start // self.block_rows
        self.n_groups = n_seq // SEQ_GROUP
        self.rows = self.tc * SEQ_GROUP

    def row_spec(self, cols):
        b0 = self.block0
        return pl.BlockSpec((self.block_rows, cols), lambda g, c: (b0 + c, 0))

    def group_spec(self, shape):
        nd = len(shape)
        return pl.BlockSpec(shape, lambda g, c: (g,) + (0,) * (nd - 1))

    def grid(self):
        return (self.n_groups, self.n_chunks)


def _load_rows(ref, region):
    if region.n_seq == SEQ_GROUP:
        return ref[...]
    g = pl.program_id(0)
    parts = [ref[pl.ds(pl.multiple_of(t * region.n_seq + g * SEQ_GROUP, SEQ_GROUP), SEQ_GROUP), :]
             for t in range(region.tc)]
    return jnp.concatenate(parts, axis=0)


def _store_rows(ref, val, region):
    if region.n_seq == SEQ_GROUP:
        ref[...] = val
        return
    g = pl.program_id(0)
    for t in range(region.tc):
        ref[pl.ds(pl.multiple_of(t * region.n_seq + g * SEQ_GROUP, SEQ_GROUP), SEQ_GROUP), :] = (
            val[t * SEQ_GROUP:(t + 1) * SEQ_GROUP])


def _const2(shape):
    nd = len(shape)
    return pl.BlockSpec(shape, lambda g, c: (0,) * nd)


def _recurrent_call(kernel, region, row_inputs, group_inputs, const_inputs, out_cols, group_outs, scratch,
                    prev_out, name):
    in_specs, args = [], []
    for arr, cols in row_inputs:
        in_specs.append(region.row_spec(cols))
        args.append(arr)
    for arr, shp in group_inputs:
        in_specs.append(region.group_spec(shp))
        args.append(arr)
    for arr in const_inputs:
        in_specs.append(_const2(arr.shape))
        args.append(arr)
    n_total = row_inputs[0][0].shape[0]
    out_shape = [jax.ShapeDtypeStruct((n_total, out_cols), F32)]
    out_specs = [region.row_spec(out_cols)]
    for full_shape, shp in group_outs:
        out_shape.append(jax.ShapeDtypeStruct(full_shape, F32))
        out_specs.append(region.group_spec(shp))
    if prev_out is None:
        prev_out = jnp.zeros((n_total, out_cols), F32)
    in_specs.append(pl.BlockSpec(memory_space=pl.ANY))
    args.append(prev_out)
    aliases = {len(args) - 1: 0}
    return pl.pallas_call(
        functools.partial(kernel, region=region, has_prev=True),
        out_shape=out_shape, grid=region.grid(), in_specs=in_specs, out_specs=out_specs,
        scratch_shapes=scratch, input_output_aliases=aliases,
        compiler_params=_params(("arbitrary", "arbitrary")), name=name,
    )(*args)


def _s5_kernel(u_ref, h0r_ref, h0i_ref, ar_ref, ai_ref, wb_ref, wcr_ref, wci_ref, d_ref, wglu_ref, *rest,
               region, has_prev):
    if has_prev:
        rest = rest[1:]
    y_ref, hr_out, hi_out, bur, bui, sr, si = rest
    c = pl.program_id(1)
    n_state = sr.shape[1]

    @pl.when(c == 0)
    def _():
        sr[...] = h0r_ref[...]
        si[...] = h0i_ref[...]

    u = _load_rows(u_ref, region)
    ub = _bf(u)
    bur[...] = _dot(ub, wb_ref[:, :n_state])
    bui[...] = _dot(ub, wb_ref[:, n_state:])

    lane_blk = 4 * V7X_LANES
    for cb in range(n_state // lane_blk):
        cols = slice(cb * lane_blk, (cb + 1) * lane_blk)
        ar = jnp.broadcast_to(ar_ref[:, cols], (SEQ_GROUP, lane_blk))
        ai = jnp.broadcast_to(ai_ref[:, cols], (SEQ_GROUP, lane_blk))

        def step(t, carry, cols=cols, ar=ar, ai=ai):
            hr, hi = carry
            rows = pl.ds(pl.multiple_of(t * SEQ_GROUP, SEQ_GROUP), SEQ_GROUP)
            nr = ar * hr - ai * hi + bur[rows, cols]
            ni = ar * hi + ai * hr + bui[rows, cols]
            bur[rows, cols] = nr
            bui[rows, cols] = ni
            return nr, ni

        hr, hi = lax.fori_loop(0, region.tc, step, (sr[:, cols], si[:, cols]))
        sr[:, cols] = hr
        si[:, cols] = hi

    y = _dot(_bf(bur[...]), wcr_ref[...]) - _dot(_bf(bui[...]), wci_ref[...]) + d_ref[...] * u
    g = _gelu_tanh(y)
    _store_rows(y_ref, g * _sigmoid(_dot(_bf(g), wglu_ref[...])), region)
    hr_out[...] = sr[...]
    hi_out[...] = si[...]


def _s5_weights(lam_re, lam_im, log_step, b_re, b_im, c_re, c_im):
    g, p, ch = b_re.shape
    dt = jnp.exp(log_step)[:, None]
    mag = jnp.exp(lam_re * dt)
    ab_re = mag * jnp.cos(lam_im * dt)
    ab_im = mag * jnp.sin(lam_im * dt)
    den = lam_re * lam_re + lam_im * lam_im
    zr = ((ab_re - 1.0) * lam_re + ab_im * lam_im) / den
    zi = (ab_im * lam_re - (ab_re - 1.0) * lam_im) / den
    bb_re = zr[..., None] * b_re - zi[..., None] * b_im
    bb_im = zr[..., None] * b_im + zi[..., None] * b_re
    eye = jnp.eye(g, dtype=F32)
    blk_b = lambda m: jnp.einsum('gpc,gh->gchp', m, eye).reshape(g * ch, g * p)
    blk_c = lambda m: jnp.einsum('gcp,gh->gphc', m, eye).reshape(g * p, g * ch)
    wb = _bf(jnp.concatenate([blk_b(bb_re), blk_b(bb_im)], axis=1))
    return (ab_re.reshape(1, g * p), ab_im.reshape(1, g * p), wb, _bf(blk_c(c_re)), _bf(blk_c(c_im)))


def _s5_branch(u_all, region, h0r, h0i, wts, d_skip, w_glu_bf, prev_out):
    ar, ai, wb, wcr, wci = wts
    width = u_all.shape[1]
    n_state = ar.shape[1]
    scratch = [pltpu.VMEM((region.rows, n_state), F32), pltpu.VMEM((region.rows, n_state), F32),
               pltpu.VMEM((SEQ_GROUP, n_state), F32), pltpu.VMEM((SEQ_GROUP, n_state), F32)]
    st = (SEQ_GROUP, n_state)
    return _recurrent_call(
        _s5_kernel, region, [(u_all, width)], [(h0r, st), (h0i, st)],
        [ar, ai, wb, wcr, wci, d_skip.reshape(1, width), w_glu_bf], width,
        [((region.n_seq, n_state), st), ((region.n_seq, n_state), st)], scratch, prev_out, "s5_branch")


def _hgrn_kernel(x_ref, s0_ref, lb_ref, gw_ref, *rest, region, has_prev):
    if has_prev:
        rest = rest[1:]
    y_ref, sout_ref, st_ref, q_s, k_s, v_s, lf_s, bc_s, qt_s, kh_s, o_s, vt_s, tr_s = rest
    c = pl.program_id(1)
    width = q_s.shape[1]
    n_heads = width // HG_KDIM
    sub_rows = HG_SUB * SEQ_GROUP
    rows_valid = region.rows
    rows_pad = q_s.shape[0]
    steps_valid = min(HG_SUB, region.tc)

    @pl.when(c == 0)
    def _():
        for b in range(SEQ_GROUP):
            for h in range(n_heads):
                st_ref[b, h] = s0_ref[b, h].T

    x = _load_rows(x_ref, region)
    if rows_pad > rows_valid:
        x = jnp.concatenate([x, jnp.zeros((rows_pad - rows_valid, x.shape[1]), F32)], axis=0)
    lb = lb_ref[...]
    z = x[:, width:2 * width]
    sig = _sigmoid(z)
    lf = jnp.log(lb + (1.0 - lb) * sig)
    kk = (1.0 - lb) * (1.0 - sig)
    if rows_pad > rows_valid:
        valid = lax.broadcasted_iota(jnp.int32, (rows_pad, 1), 0) < rows_valid
        lf = jnp.where(valid, lf, 0.0)
        kk = jnp.where(valid, kk, 0.0)
    q_s[...] = _silu(x[:, :width])
    k_s[...] = kk
    v_s[...] = x[:, 2 * width:3 * width]
    lf_s[...] = lf

    row_seq = lax.broadcasted_iota(jnp.int32, (sub_rows, 1), 0) % SEQ_GROUP
    lane_seq = lax.broadcasted_iota(jnp.int32, (1, sub_rows), 1) % SEQ_GROUP

    def sub_chunk(sc, _):
        r0 = pl.multiple_of(sc * sub_rows, sub_rows)
        slab = lambda t: pl.ds(pl.multiple_of(r0 + t * SEQ_GROUP, SEQ_GROUP), SEQ_GROUP)
        bcum = jnp.zeros((SEQ_GROUP, width), F32)
        for t in range(HG_SUB):
            bcum = bcum + lf_s[slab(t), :]
            bc_s[slab(t), :] = bcum
        blast = bcum
        for t in range(HG_SUB):
            bt = bc_s[slab(t), :]
            qt_s[slab(t), :] = q_s[slab(t), :] * jnp.exp(bt)
            kh_s[slab(t), :] = k_s[slab(t), :] * jnp.exp(blast - bt)
        for t in range(steps_valid):
            qt = q_s[slab(t), :]
            bt = bc_s[slab(t), :]
            acc = [jnp.zeros((SEQ_GROUP, HG_KDIM), F32) for _ in range(n_heads)]
            for s in range(t + 1):
                p = qt * k_s[slab(s), :] * jnp.exp(bt - bc_s[slab(s), :])
                vs = v_s[slab(s), :]
                for h in range(n_heads):
                    hs = slice(h * HG_KDIM, (h + 1) * HG_KDIM)
                    acc[h] = acc[h] + jnp.sum(p[:, hs], axis=-1, keepdims=True) * vs[:, hs]
            o_s[slab(t), :] = jnp.concatenate(acc, axis=-1)
        for t in range(steps_valid, HG_SUB):
            o_s[slab(t), :] = jnp.zeros((SEQ_GROUP, width), F32)
        dl = jnp.exp(blast)
        rows = pl.ds(r0, sub_rows)
        for h in range(n_heads):
            hs = slice(h * HG_KDIM, (h + 1) * HG_KDIM)
            qh = _bf(qt_s[rows, hs])
            kh = _bf(kh_s[rows, hs])
            vt_s[...] = v_s[rows, hs].T
            vt = vt_s[...]
            o_int = jnp.zeros((sub_rows, HG_KDIM), F32)
            for b in range(SEQ_GROUP):
                st = st_ref[b, h]
                tr_s[...] = st.T
                ob = _dot(qh, _bf(tr_s[...]))
                o_int = jnp.where(row_seq == b, ob, o_int)
                upd = _dot(_bf(jnp.where(lane_seq == b, vt, 0.0)), kh)
                st_ref[b, h] = st * dl[b:b + 1, hs] + upd
            o_s[rows, hs] = o_s[rows, hs] + o_int
        return 0

    lax.fori_loop(0, rows_pad // sub_rows, sub_chunk, 0)

    o = o_s[...][:rows_valid]
    g_raw = x[:rows_valid, 3 * width:4 * width]
    outs = []
    for h in range(n_heads):
        hs = slice(h * HG_KDIM, (h + 1) * HG_KDIM)
        oh = o[:, hs]
        oh = oh * lax.rsqrt(jnp.mean(oh * oh, axis=-1, keepdims=True) + NORM_EPS) * gw_ref[...]
        outs.append(oh * _silu(g_raw[:, hs]))
    _store_rows(y_ref, jnp.concatenate(outs, axis=-1), region)

    @pl.when(c == region.n_chunks - 1)
    def _():
        for b in range(SEQ_GROUP):
            for h in range(n_heads):
                sout_ref[b, h] = st_ref[b, h].T


def _hgrn_branch(hg_all, region, s0, lb, gnorm_w, prev_out):
    width = hg_all.shape[1] // 4
    n_heads = width // HG_KDIM
    sub_rows = HG_SUB * SEQ_GROUP
    rows_pad = max(region.rows, sub_rows)
    assert rows_pad % sub_rows == 0
    st = (SEQ_GROUP, n_heads, HG_KDIM, HG_KDIM)
    scratch = ([pltpu.VMEM(st, F32)] + [pltpu.VMEM((rows_pad, width), F32) for _ in range(8)]
               + [pltpu.VMEM((sub_rows, HG_KDIM), F32), pltpu.VMEM((HG_KDIM, HG_KDIM), F32)])
    return _recurrent_call(
        _hgrn_kernel, region, [(hg_all, 4 * width)], [(s0, st)],
        [lb.reshape(1, width), gnorm_w.reshape(1, HG_KDIM)], width,
        [((region.n_seq, n_heads, HG_KDIM, HG_KDIM), st)], scratch, prev_out, "hgrn2_branch")


def _rwkv_kernel(p_ref, prev0_ref, s0_ref, mu_ref, w0_ref, w2_ref, a0_ref, a2_ref, g2_ref, kk_ref, ka_ref,
                 rk_ref, lnw_ref, lnb_ref, ones_h_ref, ones_p_ref, imask_ref, *rest, region, has_prev):
    if has_prev:
        rest = rest[1:]
    (y_ref, sout_ref, last_ref, st_ref, prev_s, r_s, w_s, k_s, v_s, a_s, b_s, yr_s,
     pa_s, pvh_s, pvl_s, sa_s, vb_s) = rest
    c = pl.program_id(1)
    width = r_s.shape[1]
    n_pairs = width // V7X_LANES
    blk = RW_HEAD

    @pl.when(c == 0)
    def _():
        st_ref[...] = s0_ref[...]
        prev_s[...] = prev0_ref[...]

    p = _load_rows(p_ref, region)
    rows = p.shape[0]
    p_prev = jnp.concatenate([prev_s[...], p[:rows - SEQ_GROUP]], axis=0) if rows > SEQ_GROUP else prev_s[...]
    prev_s[...] = p[rows - SEQ_GROUP:]
    m = p + (p_prev - p) * mu_ref[...]
    r = m[:, :width]
    k = m[:, width:2 * width]
    v = m[:, 2 * width:3 * width]
    dwa = m[:, 3 * width:3 * width + RW_DECAY_LORA + RW_A_LORA]
    dg = m[:, 3 * width + RW_DECAY_LORA + RW_A_LORA:]
    w_log = -_softplus(-(w0_ref[...] + _dot(_bf(jnp.tanh(dwa)), w2_ref[...]))) - 0.5
    a = _sigmoid(a0_ref[...] + _dot(_bf(dwa), a2_ref[...]))
    g = _dot(_bf(_sigmoid(dg)), g2_ref[...])
    kk = k * kk_ref[...]
    kk = kk / jnp.maximum(jnp.sqrt(_dot_hilo(kk * kk, ones_h_ref[...])), 1e-12)
    k2 = k * (1.0 + (a - 1.0) * ka_ref[...])
    r_s[...] = r
    w_s[...] = jnp.exp(-jnp.exp(w_log))
    k_s[...] = k2
    v_s[...] = v
    a_s[...] = -kk
    b_s[...] = kk * a

    imask = imask_ref[...]
    ones_p = ones_p_ref[...]
    sub_iota = lax.broadcasted_iota(jnp.int32, (SEQ_GROUP, V7X_LANES), 0)

    def bc(slab, b, hp):
        return jnp.broadcast_to(slab[b:b + 1, hp * V7X_LANES:(hp + 1) * V7X_LANES], (blk, V7X_LANES))

    def step(t, _):
        rows_t = pl.ds(pl.multiple_of(t * SEQ_GROUP, SEQ_GROUP), SEQ_GROUP)
        a_t, v_t = a_s[rows_t, :], v_s[rows_t, :]
        for b in range(SEQ_GROUP):
            for hp in range(n_pairs):
                rs = pl.ds((b * n_pairs + hp) * blk, blk)
                pa_s[rs, :] = _bf(st_ref[rs, :] * bc(a_t, b, hp))
                vh, vl = _split(bc(v_t, b, hp) * imask)
                pvh_s[rs, :] = vh
                pvl_s[rs, :] = vl
        sa_s[...] = _dot(pa_s[...], ones_p)
        vb_s[...] = _dot(pvh_s[...], ones_p) + _dot(pvl_s[...], ones_p)
        w_t, k_t, b_t, r_t = w_s[rows_t, :], k_s[rows_t, :], b_s[rows_t, :], r_s[rows_t, :]
        for b in range(SEQ_GROUP):
            for hp in range(n_pairs):
                rs = pl.ds((b * n_pairs + hp) * blk, blk)
                s_new = (st_ref[rs, :] * bc(w_t, b, hp) + sa_s[rs, :] * bc(b_t, b, hp)
                         + vb_s[rs, :] * bc(k_t, b, hp))
                st_ref[rs, :] = s_new
                pa_s[rs, :] = _bf(s_new * bc(r_t, b, hp))
        sa_s[...] = _dot(pa_s[...], ones_p)
        for hp in range(n_pairs):
            tile = jnp.zeros((SEQ_GROUP, V7X_LANES), F32)
            for b in range(SEQ_GROUP):
                rs = pl.ds((b * n_pairs + hp) * blk, blk)
                yrow = jnp.sum(sa_s[rs, :] * imask, axis=0, keepdims=True)
                tile = jnp.where(sub_iota == b, jnp.broadcast_to(yrow, tile.shape), tile)
            yr_s[rows_t, hp * V7X_LANES:(hp + 1) * V7X_LANES] = tile
        return 0

    lax.fori_loop(0, region.tc, step, 0)

    y = yr_s[...]
    ones_h = ones_h_ref[...]
    inv = 1.0 / RW_HEAD
    mean = _dot_hilo(y, ones_h) * inv
    yc = y - mean
    var = _dot_hilo(yc * yc, ones_h) * inv
    yn = yc * lax.rsqrt(var + RW_LN_EPS) * lnw_ref[...] + lnb_ref[...]
    bonus = _dot_hilo(r * k2 * rk_ref[...], ones_h) * v
    _store_rows(y_ref, (yn + bonus) * g, region)
    last_ref[...] = prev_s[...]

    @pl.when(c == region.n_chunks - 1)
    def _():
        sout_ref[...] = st_ref[...]


def _rwkv_consts(width):
    lane = jnp.arange(width)
    ones_h = _bf((lane[:, None] // RW_HEAD == lane[None, :] // RW_HEAD).astype(F32))
    lane_p = jnp.arange(V7X_LANES)
    ones_p = _bf((lane_p[:, None] // RW_HEAD == lane_p[None, :] // RW_HEAD).astype(F32))
    imask = (lane_p[None, :] % RW_HEAD == jnp.arange(RW_HEAD)[:, None]).astype(F32)
    return ones_h, ones_p, imask


def _rwkv_branch(pc_all, region, prev0, s0_packed, wts, consts, prev_out):
    cols = pc_all.shape[1]
    width = wts['w0'].shape[1]
    n_pairs = width // V7X_LANES
    st_rows = SEQ_GROUP * n_pairs * RW_HEAD
    vec = lambda: pltpu.VMEM((region.rows, width), F32)
    scratch = [pltpu.VMEM((st_rows, V7X_LANES), F32), pltpu.VMEM((SEQ_GROUP, cols), F32),
               vec(), vec(), vec(), vec(), vec(), vec(), vec(),
               pltpu.VMEM((st_rows, V7X_LANES), BF16), pltpu.VMEM((st_rows, V7X_LANES), BF16),
               pltpu.VMEM((st_rows, V7X_LANES), BF16),
               pltpu.VMEM((st_rows, V7X_LANES), F32), pltpu.VMEM((st_rows, V7X_LANES), F32)]
    const_inputs = [wts['mu'], wts['w0'], wts['w2'], wts['a0'], wts['a2'], wts['g2'], wts['k_k'], wts['k_a'],
                    wts['r_k'], wts['ln_w'], wts['ln_b'], *consts]
    return _recurrent_call(
        _rwkv_kernel, region, [(pc_all, cols)],
        [(prev0, (SEQ_GROUP, cols)), (s0_packed, (st_rows, V7X_LANES))], const_inputs, width,
        [((region.n_groups * st_rows, V7X_LANES), (st_rows, V7X_LANES)),
         ((region.n_seq, cols), (SEQ_GROUP, cols))], scratch, prev_out, "rwkv7_branch")


def _rwkv_pack_state(s):
    b, h, n, _ = s.shape
    return s.reshape(b, h // 2, 2, n, n).transpose(0, 1, 3, 2, 4).reshape(b * (h // 2) * n, 2 * n)


def _rwkv_unpack_state(s, b, h, n):
    return s.reshape(b, h // 2, n, 2, n).transpose(0, 1, 3, 2, 4).reshape(b, h, n, n)


def _mix_kernel(x_ref, ya_ref, yb_ref, yc_ref, gt_ref, wbr_ref, wout_ref, g1_ref, gn_ref, sh_ref, sc_ref,
                wr_ref, br_ref, x1_ref, h2p_ref, topi_ref, topw_ref):
    d = x_ref.shape[1]
    mixed = None
    for i, y_ref in enumerate((ya_ref, yb_ref, yc_ref)):
        term = _sigmoid(gt_ref[:, i * d:(i + 1) * d]) * _dot(_bf(y_ref[...]), wbr_ref[i])
        mixed = term if mixed is None else mixed + term
    x1 = x_ref[...] + g1_ref[...] * _dot(_bf(mixed), wout_ref[...])
    x1_ref[...] = x1
    h2 = _rms(x1, gn_ref[...]) * (1.0 + sc_ref[...]) + sh_ref[...]

    bits = pltpu.bitcast(_bf(h2).astype(F32), jnp.uint32)
    half = d // 2
    h2p_ref[...] = (bits[:, :half] >> 16) | (bits[:, half:] & jnp.uint32(0xFFFF0000))

    logits = _dot3(h2, wr_ref[...]) + br_ref[...]
    lane = lax.broadcasted_iota(jnp.int32, logits.shape, 1).astype(F32)
    vals, idxs = [], []
    for _ in range(TOP_K):
        mx = jnp.max(logits, axis=-1, keepdims=True)
        ix = jnp.min(jnp.where(logits == mx, lane, float(V7X_LANES)), axis=-1, keepdims=True)
        vals.append(mx)
        idxs.append(ix)
        logits = jnp.where(lane == ix, NEG_BIG, logits)
    exps = [jnp.exp(vv - vals[0]) for vv in vals]
    tot = exps[0]
    for e in exps[1:]:
        tot = tot + e
    ti = jnp.zeros(lane.shape, F32)
    tw = jnp.zeros(lane.shape, F32)
    for k in range(TOP_K):
        ti = jnp.where(lane == float(k), idxs[k], ti)
        tw = jnp.where(lane == float(k), exps[k] / tot, tw)
    topi_ref[...] = ti.astype(jnp.int32)
    topw_ref[...] = tw


def _mix(x, ya, yb, yc, gt, wbr_bf, wout_bf, modt, g_norm2, wr_pad, br_pad, layer, n_prompt_tiles):
    n, d = x.shape
    w = ya.shape[1]
    row = lambda c: pl.BlockSpec((ROW_TILE, c), lambda i: (i, 0))
    in_specs = [row(d), row(w), row(w), row(w), row(3 * d),
                pl.BlockSpec((None, 3, w, d), lambda i: (layer, 0, 0, 0)),
                pl.BlockSpec((None, d, d), lambda i: (layer, 0, 0)),
                _mod_spec(layer, 2, d, n_prompt_tiles),
                pl.BlockSpec((None, 1, d), lambda i: (layer, 0, 0)),
                _mod_spec(layer, 3, d, n_prompt_tiles), _mod_spec(layer, 4, d, n_prompt_tiles),
                pl.BlockSpec((None, d, V7X_LANES), lambda i: (layer, 0, 0)),
                pl.BlockSpec((None, 1, V7X_LANES), lambda i: (layer, 0, 0))]
    out_shape = [jax.ShapeDtypeStruct((n, d), F32), jax.ShapeDtypeStruct((n, d // 2), jnp.uint32),
                 jax.ShapeDtypeStruct((n, V7X_LANES), jnp.int32), jax.ShapeDtypeStruct((n, V7X_LANES), F32)]
    out_specs = [row(d), row(d // 2), row(V7X_LANES), row(V7X_LANES)]
    return pl.pallas_call(
        _mix_kernel, out_shape=out_shape, grid=(n // ROW_TILE,), in_specs=in_specs, out_specs=out_specs,
        compiler_params=_params(("arbitrary",)), name="mix_norm_router",
    )(x, ya, yb, yc, gt, wbr_bf, wout_bf, modt, g_norm2, modt, modt, wr_pad, br_pad)


def _moe_kernel(be_ref, nact_ref, tok_ref, dest_ref, h2p_ref, wgu_ref, bgu_ref, wd_ref, bd_ref, yt_in_ref,
                yt_ref, xbuf, ybuf, sem):
    del be_ref, yt_in_ref
    j = pl.program_id(0)
    n_blocks = pl.num_programs(0)
    n_act = nact_ref[0]
    slot = j % 2
    d_ff = wd_ref.shape[0]

    def row_copy(s, r, dst_row):
        return pltpu.make_async_copy(ybuf.at[s, pl.ds(r, 1), :], yt_ref.at[pl.ds(dst_row, 1), :], sem.at[s])

    def wait_block(s):
        for r in range(MOE_BLOCK):
            row_copy(s, 0, 0).wait()

    @pl.when(jnp.logical_and(j >= 2, j - 2 < n_act))
    def _():
        wait_block(slot)

    @pl.when(j < n_act)
    def _():
        for r in range(MOE_BLOCK):
            xbuf[pl.ds(r, 1), :] = h2p_ref[pl.ds(tok_ref[0, r], 1), :]
        words = xbuf[...]
        lo = pltpu.bitcast(words << 16, F32)
        hi = pltpu.bitcast(words & jnp.uint32(0xFFFF0000), F32)
        x = _bf(jnp.concatenate([lo, hi], axis=1))
        gu = _dot(x, wgu_ref[...]) + bgu_ref[...]
        gate = jnp.minimum(gu[:, :d_ff], SWIGLU_LIMIT)
        up = jnp.clip(gu[:, d_ff:], -SWIGLU_LIMIT, SWIGLU_LIMIT)
        act = (up + 1.0) * gate * _sigmoid(SWIGLU_ALPHA * gate)
        ybuf[slot] = _dot(_bf(act), wd_ref[...]) + bd_ref[...]
        for r in range(MOE_BLOCK):
            row_copy(slot, r, dest_ref[0, r]).start()

    @pl.when(j == n_blocks - 1)
    def _():
        @pl.when(jnp.logical_and(j >= 1, j - 1 < n_act))
        def _():
            wait_block(1 - slot)

        @pl.when(j < n_act)
        def _():
            wait_block(slot)


def _moe_routing(topi, n):
    a_total = n * TOP_K
    flat_e = topi[:, :TOP_K].reshape(a_total)
    order = jnp.argsort(flat_e, stable=True).astype(jnp.int32)
    counts = jnp.zeros((N_EXPERTS,), jnp.int32).at[flat_e].add(1)
    padded = (counts + MOE_BLOCK - 1) // MOE_BLOCK * MOE_BLOCK
    seg_end = jnp.cumsum(padded)
    seg_start = seg_end - padded
    unp_start = jnp.cumsum(counts) - counts
    n_blocks = (a_total + N_EXPERTS * (MOE_BLOCK - 1) + MOE_BLOCK - 1) // MOE_BLOCK
    n_act = (seg_end[-1] // MOE_BLOCK).astype(jnp.int32)
    blk_start = jnp.arange(n_blocks, dtype=jnp.int32) * MOE_BLOCK
    blk_e = jnp.searchsorted(seg_end, blk_start, side='right').astype(jnp.int32)
    last_e = jnp.searchsorted(seg_end, (n_act - 1) * MOE_BLOCK, side='right').astype(jnp.int32)
    blk_e = jnp.where(blk_start < seg_end[-1], blk_e, last_e)
    pos = jnp.arange(n_blocks * MOE_BLOCK, dtype=jnp.int32)
    e_of_p = jnp.repeat(blk_e, MOE_BLOCK)
    rank = pos - seg_start[e_of_p]
    valid = jnp.logical_and(rank < counts[e_of_p], pos < seg_end[-1])
    asg = order[jnp.clip(unp_start[e_of_p] + rank, 0, a_total - 1)]
    tok = jnp.where(valid, asg // TOP_K, 0)
    dest = jnp.where(valid, (asg % TOP_K) * n + asg // TOP_K, a_total + pos)
    return (blk_e, n_act.reshape(1), tok.reshape(n_blocks, 1, MOE_BLOCK),
            dest.reshape(n_blocks, 1, MOE_BLOCK), n_blocks)


def _moe(h2p, topi, wgu_bf, bgu, wd_bf, bd, layer):
    n, half = h2p.shape
    d = 2 * half
    d_ff = wd_bf.shape[2]
    blk_e, n_act, tok, dest, n_blocks = _moe_routing(topi, n)
    yt_rows = n * TOP_K + n_blocks * MOE_BLOCK
    yt0 = jnp.zeros((yt_rows, d), F32)
    smem_blk = lambda: pl.BlockSpec((None, 1, MOE_BLOCK), lambda j, be, na: (j, 0, 0), memory_space=pltpu.SMEM)
    grid_spec = pltpu.PrefetchScalarGridSpec(
        num_scalar_prefetch=2, grid=(n_blocks,),
        in_specs=[smem_blk(), smem_blk(),
                  pl.BlockSpec((n, half), lambda j, be, na: (0, 0), pipeline_mode=pl.Buffered(1)),
                  pl.BlockSpec((None, None, d, 2 * d_ff), lambda j, be, na: (layer, be[j], 0, 0)),
                  pl.BlockSpec((None, None, 1, 2 * d_ff), lambda j, be, na: (layer, be[j], 0, 0)),
                  pl.BlockSpec((None, None, d_ff, d), lambda j, be, na: (layer, be[j], 0, 0)),
                  pl.BlockSpec((None, None, 1, d), lambda j, be, na: (layer, be[j], 0, 0)),
                  pl.BlockSpec(memory_space=pl.ANY)],
        out_specs=pl.BlockSpec(memory_space=pl.ANY),
        scratch_shapes=[pltpu.VMEM((MOE_BLOCK, half), jnp.uint32), pltpu.VMEM((2, MOE_BLOCK, d), F32),
                        pltpu.SemaphoreType.DMA((2,))])
    return pl.pallas_call(
        _moe_kernel, out_shape=jax.ShapeDtypeStruct((yt_rows, d), F32), grid_spec=grid_spec,
        input_output_aliases={9: 0},
        compiler_params=_params(("arbitrary",)), name="moe_experts",
    )(blk_e, n_act, tok, dest, h2p, wgu_bf, bgu, wd_bf, bd, yt0)


def _final_kernel(x1_ref, g2_ref, tw_ref, y0, y1, y2, y3, g_ref, o_ref):
    x = _combine_moe(x1_ref[...], g2_ref[...], tw_ref[...], [y0[...], y1[...], y2[...], y3[...]])
    o_ref[...] = _rms(x, g_ref[...])


def _final(x1, modt, topw, yt, g_final, layer, n_prompt_tiles):
    n, d = x1.shape
    n_tiles = n // ROW_TILE
    row = lambda c: pl.BlockSpec((ROW_TILE, c), lambda i: (i, 0))
    in_specs = [row(d), _mod_spec(layer, 5, d, n_prompt_tiles), row(V7X_LANES)]
    args = [x1, modt, topw]
    for k in range(TOP_K):
        in_specs.append(pl.BlockSpec((ROW_TILE, d), functools.partial(lambda i, k: (k * n_tiles + i, 0), k=k)))
        args.append(yt)
    in_specs.append(pl.BlockSpec((1, d), lambda i: (0, 0)))
    args.append(g_final.reshape(1, d))
    return pl.pallas_call(
        _final_kernel, out_shape=jax.ShapeDtypeStruct((n, d), F32), grid=(n_tiles,),
        in_specs=in_specs, out_specs=row(d), compiler_params=_params(("arbitrary",)), name="final_norm",
    )(*args)


def _time_major(x):
    b, t, d = x.shape
    return x.transpose(1, 0, 2).reshape(t * b, d)


def _tile_mod(mod, n_seq):
    return jnp.tile(mod, (1, ROW_TILE // n_seq, 1))


def kernel(x_prompt, x_sample, state_s5_re, state_s5_im, state_hgrn, state_rwkv, state_rwkv_shift, c_prompt, c_sample, g_norm1, g_norm2, w_ada, b_ada, w_in, s5_lam_re, s5_lam_im, s5_log_step, s5_b_re, s5_b_im, s5_c_re, s5_c_im, s5_d, s5_w_glu, hg_lb, hg_gnorm, rw_mu, rw_w0, rw_w2, rw_a0, rw_a2, rw_g2, rw_k_k, rw_k_a, rw_r_k, rw_ln_w, rw_ln_b, w_branch, w_out, w_router, b_router, w_gate_up, b_gate_up, w_down, b_down, g_final):
    bp, tp, d = x_prompt.shape
    bs, ts, _ = x_sample.shape
    n_layers = w_in.shape[0]
    width = d // 2
    n_p, n_s = bp * tp, bs * ts
    n = n_p + n_s
    assert n_p % ROW_TILE == 0 and n_s % ROW_TILE == 0 and ROW_TILE % bp == 0 and ROW_TILE % bs == 0
    n_prompt_tiles = n_p // ROW_TILE
    assert n_s // ROW_TILE <= n_prompt_tiles
    reg_p = _Region(0, tp, bp)
    reg_s = _Region(n_p, ts, bs)

    x = jnp.concatenate([_time_major(x_prompt), _time_major(x_sample)], axis=0)
    mod = _ada_all(jnp.concatenate([c_prompt, c_sample], axis=0), w_ada, b_ada)
    modt = jnp.stack([_tile_mod(mod[:, :bp], bp), _tile_mod(mod[:, bp:], bs)], axis=1)

    splits = [(0, width), (width, 5 * width), (5 * width, 5 * width + rw_mu.shape[1]),
              (5 * width + rw_mu.shape[1], w_in.shape[2])]
    w_in_bf = _bf(w_in)
    wbr_bf, wout_bf, wglu_bf = _bf(w_branch), _bf(w_out), _bf(s5_w_glu)
    wgu_bf, wd_bf = _bf(w_gate_up), _bf(w_down)
    bgu = b_gate_up.reshape(n_layers, N_EXPERTS, 1, -1)
    bd = b_down.reshape(n_layers, N_EXPERTS, 1, d)
    wr_pad = jnp.pad(w_router, ((0, 0), (0, 0), (0, V7X_LANES - N_EXPERTS)))
    br_pad = jnp.pad(b_router, ((0, 0), (0, V7X_LANES - N_EXPERTS)), constant_values=NEG_BIG)[:, None, :]
    g1 = g_norm1[:, None, :]
    g2 = g_norm2[:, None, :]
    lb_soft = jax.nn.softmax(hg_lb, axis=0)
    lb_all = jnp.cumsum(lb_soft, axis=0) - lb_soft[0:1]
    rw_consts = _rwkv_consts(width)
    pad_lo = lambda m: _bf(jnp.pad(m, ((0, RW_A_LORA), (0, 0))))
    pad_hi = lambda m: _bf(jnp.pad(m, ((RW_DECAY_LORA, 0), (0, 0))))
    n_heads_rw = width // RW_HEAD

    zeros = lambda *s: jnp.zeros(s, F32)
    outs_p = {k: [] for k in ('s5r', 's5i', 'hg', 'rw', 'sh')}
    outs_s = {k: [] for k in ('s5r', 's5i', 'hg', 'rw', 'sh')}
    n_state = (width // S5_GROUP_CH) * S5_STATE

    moe_in = None
    for l in range(n_layers):
        x, (u_a, hg, pc, gt) = _inproj(x, moe_in, modt, g1, w_in_bf, l, n_prompt_tiles, splits)

        s5w = _s5_weights(s5_lam_re[l], s5_lam_im[l], s5_log_step[l], s5_b_re[l], s5_b_im[l],
                          s5_c_re[l], s5_c_im[l])
        ya, hr_p, hi_p = _s5_branch(u_a, reg_p, zeros(bp, n_state), zeros(bp, n_state), s5w, s5_d[l],
                                    wglu_bf[l], None)
        ya, hr_s, hi_s = _s5_branch(u_a, reg_s, state_s5_re[l].reshape(bs, n_state),
                                    state_s5_im[l].reshape(bs, n_state), s5w, s5_d[l], wglu_bf[l], ya)

        n_hg = width // HG_KDIM
        yb, hg_p = _hgrn_branch(hg, reg_p, zeros(bp, n_hg, HG_KDIM, HG_KDIM), lb_all[l], hg_gnorm[l], None)
        yb, hg_s = _hgrn_branch(hg, reg_s, state_hgrn[l], lb_all[l], hg_gnorm[l], yb)

        rww = dict(mu=rw_mu[l][None], w0=rw_w0[l][None], w2=pad_lo(rw_w2[l]), a0=rw_a0[l][None],
                   a2=pad_hi(rw_a2[l]), g2=_bf(rw_g2[l]), k_k=rw_k_k[l][None], k_a=rw_k_a[l][None],
                   r_k=rw_r_k[l].reshape(1, width), ln_w=rw_ln_w[l][None], ln_b=rw_ln_b[l][None])
        cols = pc.shape[1]
        yc, rw_p, sh_p = _rwkv_branch(pc, reg_p, zeros(bp, cols),
                                      zeros(bp * (n_heads_rw // 2) * RW_HEAD, 2 * RW_HEAD), rww, rw_consts, None)
        yc, rw_s, sh_s = _rwkv_branch(pc, reg_s, state_rwkv_shift[l], _rwkv_pack_state(state_rwkv[l]),
                                      rww, rw_consts, yc)

        x1, h2p, topi, topw = _mix(x, ya, yb, yc, gt, wbr_bf, wout_bf, modt, g2, wr_pad, br_pad, l,
                                   n_prompt_tiles)
        yt = _moe(h2p, topi, wgu_bf, bgu, wd_bf, bd, l)
        x = x1
        moe_in = (topw, yt)

        for dst, vals in ((outs_p, (hr_p, hi_p, hg_p, rw_p, sh_p)), (outs_s, (hr_s, hi_s, hg_s, rw_s, sh_s))):
            for key, val in zip(('s5r', 's5i', 'hg', 'rw', 'sh'), vals):
                dst[key].append(val)

    y = _final(x, modt, moe_in[0], moe_in[1], g_final, n_layers - 1, n_prompt_tiles)
    y_prompt = y[:n_p].reshape(tp, bp, d).transpose(1, 0, 2)
    y_sample = y[n_p:].reshape(ts, bs, d).transpose(1, 0, 2)

    def pack(o, b):
        g = width // S5_GROUP_CH
        return (jnp.stack(o['s5r']).reshape(n_layers, b, g, S5_STATE),
                jnp.stack(o['s5i']).reshape(n_layers, b, g, S5_STATE),
                jnp.stack(o['hg']),
                jnp.stack([_rwkv_unpack_state(s, b, n_heads_rw, RW_HEAD) for s in o['rw']]),
                jnp.stack(o['sh']))

    return (y_prompt, y_sample) + pack(outs_p, bp) + pack(outs_s, bs)
```

```python
import functools
import math

import jax
import jax.numpy as jnp
from jax import lax
from jax.experimental import pallas as pl
from jax.experimental.pallas import tpu as pltpu

F32 = jnp.float32
BF16 = jnp.bfloat16

V7X_SUBLANES = 8
V7X_LANES = 128
V7X_VMEM_LIMIT = 60 * 1024 * 1024

ROW_TILE = 256
CHUNK_ROWS = 512
SEQ_GROUP = V7X_SUBLANES
HG_SUB = 16
MOE_BLOCK = 256

S5_GROUP_CH = 16
S5_STATE = 64
HG_KDIM = 128
RW_HEAD = 64
RW_DECAY_LORA = 64
RW_A_LORA = 64
RW_GATE_LORA = 128
RW_LN_EPS = 64e-5
N_EXPERTS = 32
TOP_K = 4
SWIGLU_LIMIT = 7.0
SWIGLU_ALPHA = 1.702
NORM_EPS = 1e-6
NEG_BIG = -1e30


def _bf(x):
    return x.astype(BF16)


def _dot(a, b):
    return jnp.dot(a, b, preferred_element_type=F32)


def _split(x):
    hi = _bf(x)
    lo = _bf(x - hi.astype(F32))
    return hi, lo


def _dot_hilo(x, w_bf):
    hi, lo = _split(x)
    return _dot(hi, w_bf) + _dot(lo, w_bf)


def _dot3(x, w):
    xh, xl = _split(x)
    wh, wl = _split(w)
    return _dot(xh, wh) + _dot(xh, wl) + _dot(xl, wh)


def _sigmoid(x):
    return 1.0 / (1.0 + jnp.exp(-x))


def _silu(x):
    return x * _sigmoid(x)


def _softplus(x):
    return jnp.maximum(x, 0.0) + jnp.log(1.0 + jnp.exp(-jnp.abs(x)))


def _gelu_tanh(x):
    return 0.5 * x * (1.0 + jnp.tanh(math.sqrt(2.0 / math.pi) * (x + 0.044715 * (x * x * x))))


def _rms(x, g):
    return x * lax.rsqrt(jnp.mean(x * x, axis=-1, keepdims=True) + NORM_EPS) * g


def _params(sem):
    return pltpu.CompilerParams(dimension_semantics=sem, vmem_limit_bytes=V7X_VMEM_LIMIT)


def _const_spec(shape):
    nd = len(shape)
    return pl.BlockSpec(shape, lambda *_: (0,) * nd, pipeline_mode=pl.Buffered(1))


def _ada_kernel(c_ref, w_ref, b_ref, o_ref):
    c = c_ref[...]
    o_ref[...] = _dot3(_silu(c), w_ref[...]) + b_ref[...]


def _ada_all(c_all, w_ada, b_ada):
    n_layers, d, d6 = w_ada.shape
    n = c_all.shape[0]
    col = d6 // 4
    return pl.pallas_call(
        _ada_kernel,
        out_shape=jax.ShapeDtypeStruct((n_layers, n, d6), F32),
        grid=(n_layers, d6 // col),
        in_specs=[pl.BlockSpec((n, d), lambda l, j: (0, 0)),
                  pl.BlockSpec((None, d, col), lambda l, j: (l, 0, j)),
                  pl.BlockSpec((None, 1, col), lambda l, j: (l, 0, j))],
        out_specs=pl.BlockSpec((None, n, col), lambda l, j: (l, 0, j)),
        compiler_params=_params(("arbitrary", "arbitrary")),
        name="ada_mod",
    )(c_all, w_ada, b_ada.reshape(n_layers, 1, d6))


def _combine_moe(x1, g2, tw, ys):
    acc = tw[:, 0:1] * ys[0]
    for k in range(1, TOP_K):
        acc = acc + tw[:, k:k + 1] * ys[k]
    return x1 + g2 * acc


def _inproj_kernel(*refs, has_moe, splits):
    if has_moe:
        x1_ref, g2_ref, tw_ref = refs[:3]
        y_refs = refs[3:3 + TOP_K]
        refs = refs[3 + TOP_K:]
        x = _combine_moe(x1_ref[...], g2_ref[...], tw_ref[...], [r[...] for r in y_refs])
    else:
        x = refs[0][...]
        refs = refs[1:]
    sh_ref, sc_ref, g_ref, w_ref = refs[:4]
    outs = refs[4:]
    if has_moe:
        outs[0][...] = x
        outs = outs[1:]
    h = _rms(x, g_ref[...]) * (1.0 + sc_ref[...]) + sh_ref[...]
    hb = _bf(h)
    for o_ref, (lo, hi) in zip(outs, splits):
        o_ref[...] = _dot(hb, w_ref[:, lo:hi])


def _mod_spec(layer, piece, d, n_prompt_tiles):
    return pl.BlockSpec((None, None, ROW_TILE, d), lambda i: (layer, i // n_prompt_tiles, 0, piece))


def _inproj(x_in, moe_in, modt, g_norm1, w_in_bf, layer, n_prompt_tiles, splits):
    n, d = x_in.shape
    n_tiles = n // ROW_TILE
    row = lambda c: pl.BlockSpec((ROW_TILE, c), lambda i: (i, 0))
    in_specs, args = [row(d)], [x_in]
    has_moe = moe_in is not None
    if has_moe:
        topw, yt = moe_in
        in_specs += [_mod_spec(layer - 1, 5, d, n_prompt_tiles), row(V7X_LANES)]
        args += [modt, topw]
        for k in range(TOP_K):
            in_specs.append(pl.BlockSpec((ROW_TILE, d), functools.partial(lambda i, k: (k * n_tiles + i, 0), k=k)))
            args.append(yt)
    in_specs += [_mod_spec(layer, 0, d, n_prompt_tiles), _mod_spec(layer, 1, d, n_prompt_tiles),
                 pl.BlockSpec((None, 1, d), lambda i: (layer, 0, 0)),
                 pl.BlockSpec((None,) + w_in_bf.shape[1:], lambda i: (layer, 0, 0), pipeline_mode=pl.Buffered(1))]
    args += [modt, modt, g_norm1, w_in_bf]
    widths = [hi - lo for lo, hi in splits]
    out_shape = [jax.ShapeDtypeStruct((n, c), F32) for c in widths]
    out_specs = [row(c) for c in widths]
    if has_moe:
        out_shape = [jax.ShapeDtypeStruct((n, d), F32)] + out_shape
        out_specs = [row(d)] + out_specs
    outs = pl.pallas_call(
        functools.partial(_inproj_kernel, has_moe=has_moe, splits=splits),
        out_shape=out_shape, grid=(n_tiles,), in_specs=in_specs, out_specs=out_specs,
        compiler_params=_params(("arbitrary",)), name="norm_inproj",
    )(*args)
    if has_moe:
        return outs[0], outs[1:]
    return x_in, outs


class _Region:
    def __init__(self, row_start, n_steps, n_seq):
        self.row_start, self.n_steps, self.n_seq = row_start, n_steps, n_seq
        assert n_seq % SEQ_GROUP == 0
        if n_seq == SEQ_GROUP:
            self.tc = min(n_steps, CHUNK_ROWS // SEQ_GROUP)
            assert n_steps % self.tc == 0
            self.block_rows = self.tc * SEQ_GROUP
            self.n_chunks = n_steps // self.tc
        else:
            self.tc = n_steps
            self.block_rows = n_steps * n_seq
            self.n_chunks = 1
        assert row_start % self.block_rows == 0
        self.block0 = row_start // self.block_rows
        self.n_groups = n_seq // SEQ_GROUP
        self.rows = self.tc * SEQ_GROUP

    def row_spec(self, cols):
        b0 = self.block0
        return pl.BlockSpec((self.block_rows, cols), lambda g, c: (b0 + c, 0))

    def group_spec(self, shape):
        nd = len(shape)
        return pl.BlockSpec(shape, lambda g, c: (g,) + (0,) * (nd - 1))

    def grid(self):
        return (self.n_groups, self.n_chunks)


def _load_rows(ref, region):
    if region.n_seq == SEQ_GROUP:
        return ref[...]
    g = pl.program_id(0)
    parts = [ref[pl.ds(pl.multiple_of(t * region.n_seq + g * SEQ_GROUP, SEQ_GROUP), SEQ_GROUP), :]
             for t in range(region.tc)]
    return jnp.concatenate(parts, axis=0)


def _store_rows(ref, val, region):
    if region.n_seq == SEQ_GROUP:
        ref[...] = val
        return
    g = pl.program_id(0)
    for t in range(region.tc):
        ref[pl.ds(pl.multiple_of(t * region.n_seq + g * SEQ_GROUP, SEQ_GROUP), SEQ_GROUP), :] = (
            val[t * SEQ_GROUP:(t + 1) * SEQ_GROUP])


def _const2(shape):
    nd = len(shape)
    return pl.BlockSpec(shape, lambda g, c: (0,) * nd)


def _recurrent_call(kernel, region, row_inputs, group_inputs, const_inputs, out_cols, group_outs, scratch,
                    prev_out, name):
    in_specs, args = [], []
    for arr, cols in row_inputs:
        in_specs.append(region.row_spec(cols))
        args.append(arr)
    for arr, shp in group_inputs:
        in_specs.append(region.group_spec(shp))
        args.append(arr)
    for arr in const_inputs:
        in_specs.append(_const2(arr.shape))
        args.append(arr)
    n_total = row_inputs[0][0].shape[0]
    out_shape = [jax.ShapeDtypeStruct((n_total, out_cols), F32)]
    out_specs = [region.row_spec(out_cols)]
    for full_shape, shp in group_outs:
        out_shape.append(jax.ShapeDtypeStruct(full_shape, F32))
        out_specs.append(region.group_spec(shp))
    if prev_out is None:
        prev_out = jnp.zeros((n_total, out_cols), F32)
    in_specs.append(pl.BlockSpec(memory_space=pl.ANY))
    args.append(prev_out)
    aliases = {len(args) - 1: 0}
    return pl.pallas_call(
        functools.partial(kernel, region=region, has_prev=True),
        out_shape=out_shape, grid=region.grid(), in_specs=in_specs, out_specs=out_specs,
        scratch_shapes=scratch, input_output_aliases=aliases,
        compiler_params=_params(("arbitrary", "arbitrary")), name=name,
    )(*args)


def _s5_kernel(u_ref, h0r_ref, h0i_ref, ar_ref, ai_ref, wb_ref, wcr_ref, wci_ref, d_ref, wglu_ref, *rest,
               region, has_prev):
    if has_prev:
        rest = rest[1:]
    y_ref, hr_out, hi_out, bur, bui, sr, si = rest
    c = pl.program_id(1)
    n_state = sr.shape[1]

    @pl.when(c == 0)
    def _():
        sr[...] = h0r_ref[...]
        si[...] = h0i_ref[...]

    u = _load_rows(u_ref, region)
    ub = _bf(u)
    bur[...] = _dot(ub, wb_ref[:, :n_state])
    bui[...] = _dot(ub, wb_ref[:, n_state:])

    lane_blk = 4 * V7X_LANES
    for cb in range(n_state // lane_blk):
        cols = slice(cb * lane_blk, (cb + 1) * lane_blk)
        ar = jnp.broadcast_to(ar_ref[:, cols], (SEQ_GROUP, lane_blk))
        ai = jnp.broadcast_to(ai_ref[:, cols], (SEQ_GROUP, lane_blk))

        def step(t, carry, cols=cols, ar=ar, ai=ai):
            hr, hi = carry
            rows = pl.ds(pl.multiple_of(t * SEQ_GROUP, SEQ_GROUP), SEQ_GROUP)
            nr = ar * hr - ai * hi + bur[rows, cols]
            ni = ar * hi + ai * hr + bui[rows, cols]
            bur[rows, cols] = nr
            bui[rows, cols] = ni
            return nr, ni

        hr, hi = lax.fori_loop(0, region.tc, step, (sr[:, cols], si[:, cols]))
        sr[:, cols] = hr
        si[:, cols] = hi

    y = _dot(_bf(bur[...]), wcr_ref[...]) - _dot(_bf(bui[...]), wci_ref[...]) + d_ref[...] * u
    g = _gelu_tanh(y)
    _store_rows(y_ref, g * _sigmoid(_dot(_bf(g), wglu_ref[...])), region)
    hr_out[...] = sr[...]
    hi_out[...] = si[...]


def _s5_weights(lam_re, lam_im, log_step, b_re, b_im, c_re, c_im):
    g, p, ch = b_re.shape
    dt = jnp.exp(log_step)[:, None]
    mag = jnp.exp(lam_re * dt)
    ab_re = mag * jnp.cos(lam_im * dt)
    ab_im = mag * jnp.sin(lam_im * dt)
    den = lam_re * lam_re + lam_im * lam_im
    zr = ((ab_re - 1.0) * lam_re + ab_im * lam_im) / den
    zi = (ab_im * lam_re - (ab_re - 1.0) * lam_im) / den
    bb_re = zr[..., None] * b_re - zi[..., None] * b_im
    bb_im = zr[..., None] * b_im + zi[..., None] * b_re
    eye = jnp.eye(g, dtype=F32)
    blk_b = lambda m: jnp.einsum('gpc,gh->gchp', m, eye).reshape(g * ch, g * p)
    blk_c = lambda m: jnp.einsum('gcp,gh->gphc', m, eye).reshape(g * p, g * ch)
    wb = _bf(jnp.concatenate([blk_b(bb_re), blk_b(bb_im)], axis=1))
    return (ab_re.reshape(1, g * p), ab_im.reshape(1, g * p), wb, _bf(blk_c(c_re)), _bf(blk_c(c_im)))


def _s5_branch(u_all, region, h0r, h0i, wts, d_skip, w_glu_bf, prev_out):
    ar, ai, wb, wcr, wci = wts
    width = u_all.shape[1]
    n_state = ar.shape[1]
    scratch = [pltpu.VMEM((region.rows, n_state), F32), pltpu.VMEM((region.rows, n_state), F32),
               pltpu.VMEM((SEQ_GROUP, n_state), F32), pltpu.VMEM((SEQ_GROUP, n_state), F32)]
    st = (SEQ_GROUP, n_state)
    return _recurrent_call(
        _s5_kernel, region, [(u_all, width)], [(h0r, st), (h0i, st)],
        [ar, ai, wb, wcr, wci, d_skip.reshape(1, width), w_glu_bf], width,
        [((region.n_seq, n_state), st), ((region.n_seq, n_state), st)], scratch, prev_out, "s5_branch")


def _hgrn_kernel(x_ref, s0_ref, lb_ref, gw_ref, *rest, region, has_prev):
    if has_prev:
        rest = rest[1:]
    y_ref, sout_ref, st_ref, q_s, k_s, v_s, lf_s, bc_s, qt_s, kh_s, o_s, vt_s, tr_s = rest
    c = pl.program_id(1)
    width = q_s.shape[1]
    n_heads = width // HG_KDIM
    sub_rows = HG_SUB * SEQ_GROUP
    rows_valid = region.rows
    rows_pad = q_s.shape[0]
    steps_valid = min(HG_SUB, region.tc)

    @pl.when(c == 0)
    def _():
        for b in range(SEQ_GROUP):
            for h in range(n_heads):
                st_ref[b, h] = s0_ref[b, h].T

    x = _load_rows(x_ref, region)
    if rows_pad > rows_valid:
        x = jnp.concatenate([x, jnp.zeros((rows_pad - rows_valid, x.shape[1]), F32)], axis=0)
    lb = lb_ref[...]
    z = x[:, width:2 * width]
    sig = _sigmoid(z)
    lf = jnp.log(lb + (1.0 - lb) * sig)
    kk = (1.0 - lb) * (1.0 - sig)
    if rows_pad > rows_valid:
        valid = lax.broadcasted_iota(jnp.int32, (rows_pad, 1), 0) < rows_valid
        lf = jnp.where(valid, lf, 0.0)
        kk = jnp.where(valid, kk, 0.0)
    q_s[...] = _silu(x[:, :width])
    k_s[...] = kk
    v_s[...] = x[:, 2 * width:3 * width]
    lf_s[...] = lf

    row_seq = lax.broadcasted_iota(jnp.int32, (sub_rows, 1), 0) % SEQ_GROUP
    lane_seq = lax.broadcasted_iota(jnp.int32, (1, sub_rows), 1) % SEQ_GROUP

    def sub_chunk(sc, _):
        r0 = pl.multiple_of(sc * sub_rows, sub_rows)
        slab = lambda t: pl.ds(pl.multiple_of(r0 + t * SEQ_GROUP, SEQ_GROUP), SEQ_GROUP)
        bcum = jnp.zeros((SEQ_GROUP, width), F32)
        for t in range(HG_SUB):
            bcum = bcum + lf_s[slab(t), :]
            bc_s[slab(t), :] = bcum
        blast = bcum
        for t in range(HG_SUB):
            bt = bc_s[slab(t), :]
            qt_s[slab(t), :] = q_s[slab(t), :] * jnp.exp(bt)
            kh_s[slab(t), :] = k_s[slab(t), :] * jnp.exp(blast - bt)
        for t in range(steps_valid):
            qt = q_s[slab(t), :]
            bt = bc_s[slab(t), :]
            acc = [jnp.zeros((SEQ_GROUP, HG_KDIM), F32) for _ in range(n_heads)]
            for s in range(t + 1):
                p = qt * k_s[slab(s), :] * jnp.exp(bt - bc_s[slab(s), :])
                vs = v_s[slab(s), :]
                for h in range(n_heads):
                    hs = slice(h * HG_KDIM, (h + 1) * HG_KDIM)
                    acc[h] = acc[h] + jnp.sum(p[:, hs], axis=-1, keepdims=True) * vs[:, hs]
            o_s[slab(t), :] = jnp.concatenate(acc, axis=-1)
        for t in range(steps_valid, HG_SUB):
            o_s[slab(t), :] = jnp.zeros((SEQ_GROUP, width), F32)
        dl = jnp.exp(blast)
        rows = pl.ds(r0, sub_rows)
        for h in range(n_heads):
            hs = slice(h * HG_KDIM, (h + 1) * HG_KDIM)
            qh = _bf(qt_s[rows, hs])
            kh = _bf(kh_s[rows, hs])
            vt_s[...] = v_s[rows, hs].T
            vt = vt_s[...]
            o_int = jnp.zeros((sub_rows, HG_KDIM), F32)
            for b in range(SEQ_GROUP):
                st = st_ref[b, h]
                tr_s[...] = st.T
                ob = _dot(qh, _bf(tr_s[...]))
                o_int = jnp.where(row_seq == b, ob, o_int)
                upd = _dot(_bf(jnp.where(lane_seq == b, vt, 0.0)), kh)
                st_ref[b, h] = st * dl[b:b + 1, hs] + upd
            o_s[rows, hs] = o_s[rows, hs] + o_int
        return 0

    lax.fori_loop(0, rows_pad // sub_rows, sub_chunk, 0)

    o = o_s[...][:rows_valid]
    g_raw = x[:rows_valid, 3 * width:4 * width]
    outs = []
    for h in range(n_heads):
        hs = slice(h * HG_KDIM, (h + 1) * HG_KDIM)
        oh = o[:, hs]
        oh = oh * lax.rsqrt(jnp.mean(oh * oh, axis=-1, keepdims=True) + NORM_EPS) * gw_ref[...]
        outs.append(oh * _silu(g_raw[:, hs]))
    _store_rows(y_ref, jnp.concatenate(outs, axis=-1), region)

    @pl.when(c == region.n_chunks - 1)
    def _():
        for b in range(SEQ_GROUP):
            for h in range(n_heads):
                sout_ref[b, h] = st_ref[b, h].T


def _hgrn_branch(hg_all, region, s0, lb, gnorm_w, prev_out):
    width = hg_all.shape[1] // 4
    n_heads = width // HG_KDIM
    sub_rows = HG_SUB * SEQ_GROUP
    rows_pad = max(region.rows, sub_rows)
    assert rows_pad % sub_rows == 0
    st = (SEQ_GROUP, n_heads, HG_KDIM, HG_KDIM)
    scratch = ([pltpu.VMEM(st, F32)] + [pltpu.VMEM((rows_pad, width), F32) for _ in range(8)]
               + [pltpu.VMEM((sub_rows, HG_KDIM), F32), pltpu.VMEM((HG_KDIM, HG_KDIM), F32)])
    return _recurrent_call(
        _hgrn_kernel, region, [(hg_all, 4 * width)], [(s0, st)],
        [lb.reshape(1, width), gnorm_w.reshape(1, HG_KDIM)], width,
        [((region.n_seq, n_heads, HG_KDIM, HG_KDIM), st)], scratch, prev_out, "hgrn2_branch")


def _rwkv_kernel(p_ref, prev0_ref, s0_ref, mu_ref, w0_ref, w2_ref, a0_ref, a2_ref, g2_ref, kk_ref, ka_ref,
                 rk_ref, lnw_ref, lnb_ref, ones_h_ref, ones_p_ref, imask_ref, *rest, region, has_prev):
    if has_prev:
        rest = rest[1:]
    (y_ref, sout_ref, last_ref, st_ref, prev_s, r_s, w_s, k_s, v_s, a_s, b_s, yr_s,
     sb_s, la_s, lv_s, sa_s, vb_s) = rest
    c = pl.program_id(1)
    width = r_s.shape[1]
    n_pairs = width // V7X_LANES
    blk = RW_HEAD

    @pl.when(c == 0)
    def _():
        for b in range(SEQ_GROUP):
            for hp in range(n_pairs):
                s_blk = jnp.concatenate([s0_ref[b, 2 * hp], s0_ref[b, 2 * hp + 1]], axis=1)
                st_ref[pl.ds((b * n_pairs + hp) * blk, blk), :] = s_blk
                sb_s[pl.ds((b * n_pairs + hp) * blk, blk), :] = _bf(s_blk)
        prev_s[...] = prev0_ref[...]

    p = _load_rows(p_ref, region)
    rows = p.shape[0]
    p_prev = jnp.concatenate([prev_s[...], p[:rows - SEQ_GROUP]], axis=0) if rows > SEQ_GROUP else prev_s[...]
    prev_s[...] = p[rows - SEQ_GROUP:]
    m = p + (p_prev - p) * mu_ref[...]
    r = m[:, :width]
    k = m[:, width:2 * width]
    v = m[:, 2 * width:3 * width]
    dwa = m[:, 3 * width:3 * width + RW_DECAY_LORA + RW_A_LORA]
    dg = m[:, 3 * width + RW_DECAY_LORA + RW_A_LORA:]
    w_log = -_softplus(-(w0_ref[...] + _dot(_bf(jnp.tanh(dwa)), w2_ref[...]))) - 0.5
    a = _sigmoid(a0_ref[...] + _dot(_bf(dwa), a2_ref[...]))
    g = _dot(_bf(_sigmoid(dg)), g2_ref[...])
    kk = k * kk_ref[...]
    kk = kk / jnp.maximum(jnp.sqrt(_dot_hilo(kk * kk, ones_h_ref[...])), 1e-12)
    k2 = k * (1.0 + (a - 1.0) * ka_ref[...])
    r_s[...] = r
    w_s[...] = jnp.exp(-jnp.exp(w_log))
    k_s[...] = k2
    v_s[...] = v
    a_s[...] = -kk
    b_s[...] = kk * a

    imask = imask_ref[...]
    ones_p = ones_p_ref[...]
    sub_iota = lax.broadcasted_iota(jnp.int32, (SEQ_GROUP, V7X_LANES), 0)

    def row(slab, b, hp, n_rows, dtype=F32):
        piece = slab[b:b + 1, hp * V7X_LANES:(hp + 1) * V7X_LANES].astype(dtype)
        return jnp.broadcast_to(piece, (n_rows, V7X_LANES))

    def step(t, _):
        rows_t = pl.ds(pl.multiple_of(t * SEQ_GROUP, SEQ_GROUP), SEQ_GROUP)
        a_t, v_t = a_s[rows_t, :], v_s[rows_t, :]
        for b in range(SEQ_GROUP):
            for hp in range(n_pairs):
                rs = pl.ds((b * n_pairs + hp) * blk, blk)
                la_s[rs, :] = sb_s[rs, :] * row(a_t, b, hp, blk, BF16)
                lv_s[rs, :] = _bf(row(v_t, b, hp, blk) * imask)
        sa_s[...] = _dot(la_s[...], ones_p)
        vb_s[...] = _dot(lv_s[...], ones_p)
        w_t, k_t, b_t, r_t = w_s[rows_t, :], k_s[rows_t, :], b_s[rows_t, :], r_s[rows_t, :]
        for b in range(SEQ_GROUP):
            for hp in range(n_pairs):
                rs = pl.ds((b * n_pairs + hp) * blk, blk)
                s_new = (st_ref[rs, :] * row(w_t, b, hp, blk) + sa_s[rs, :] * row(b_t, b, hp, blk)
                         + vb_s[rs, :] * row(k_t, b, hp, blk))
                st_ref[rs, :] = s_new
                sb = _bf(s_new)
                sb_s[rs, :] = sb
                la_s[rs, :] = sb * row(r_t, b, hp, blk, BF16)
        sa_s[...] = _dot(la_s[...], ones_p)
        for hp in range(n_pairs):
            tile = jnp.zeros((SEQ_GROUP, V7X_LANES), F32)
            for b in range(SEQ_GROUP):
                rs = pl.ds((b * n_pairs + hp) * blk, blk)
                yrow = jnp.sum(sa_s[rs, :] * imask, axis=0, keepdims=True)
                tile = jnp.where(sub_iota == b, jnp.broadcast_to(yrow, tile.shape), tile)
            yr_s[rows_t, hp * V7X_LANES:(hp + 1) * V7X_LANES] = tile
        return 0

    lax.fori_loop(0, region.tc, step, 0)

    y = yr_s[...]
    ones_h = ones_h_ref[...]
    inv = 1.0 / RW_HEAD
    mean = _dot_hilo(y, ones_h) * inv
    yc = y - mean
    var = _dot_hilo(yc * yc, ones_h) * inv
    yn = yc * lax.rsqrt(var + RW_LN_EPS) * lnw_ref[...] + lnb_ref[...]
    bonus = _dot_hilo(r * k2 * rk_ref[...], ones_h) * v
    _store_rows(y_ref, (yn + bonus) * g, region)
    last_ref[...] = prev_s[...]

    @pl.when(c == region.n_chunks - 1)
    def _():
        for b in range(SEQ_GROUP):
            for hp in range(n_pairs):
                s_blk = st_ref[pl.ds((b * n_pairs + hp) * blk, blk), :]
                sout_ref[b, 2 * hp] = s_blk[:, :RW_HEAD]
                sout_ref[b, 2 * hp + 1] = s_blk[:, RW_HEAD:]


def _rwkv_consts(width):
    lane = jnp.arange(width)
    ones_h = _bf((lane[:, None] // RW_HEAD == lane[None, :] // RW_HEAD).astype(F32))
    lane_p = jnp.arange(V7X_LANES)
    ones_p = _bf((lane_p[:, None] // RW_HEAD == lane_p[None, :] // RW_HEAD).astype(F32))
    imask = (lane_p[None, :] % RW_HEAD == jnp.arange(RW_HEAD)[:, None]).astype(F32)
    return ones_h, ones_p, imask


def _rwkv_branch(pc_all, region, prev0, s0, wts, consts, prev_out):
    cols = pc_all.shape[1]
    width = wts['w0'].shape[1]
    n_pairs = width // V7X_LANES
    n_heads = width // RW_HEAD
    st_rows = SEQ_GROUP * n_pairs * RW_HEAD
    vec = lambda: pltpu.VMEM((region.rows, width), F32)
    scratch = [pltpu.VMEM((st_rows, V7X_LANES), F32), pltpu.VMEM((SEQ_GROUP, cols), F32),
               vec(), vec(), vec(), vec(), vec(), vec(), vec(),
               pltpu.VMEM((st_rows, V7X_LANES), BF16), pltpu.VMEM((st_rows, V7X_LANES), BF16),
               pltpu.VMEM((st_rows, V7X_LANES), BF16),
               pltpu.VMEM((st_rows, V7X_LANES), F32), pltpu.VMEM((st_rows, V7X_LANES), F32)]
    const_inputs = [wts['mu'], wts['w0'], wts['w2'], wts['a0'], wts['a2'], wts['g2'], wts['k_k'], wts['k_a'],
                    wts['r_k'], wts['ln_w'], wts['ln_b'], *consts]
    st_blk = (SEQ_GROUP, n_heads, RW_HEAD, RW_HEAD)
    return _recurrent_call(
        _rwkv_kernel, region, [(pc_all, cols)],
        [(prev0, (SEQ_GROUP, cols)), (s0, st_blk)], const_inputs, width,
        [((region.n_seq, n_heads, RW_HEAD, RW_HEAD), st_blk),
         ((region.n_seq, cols), (SEQ_GROUP, cols))], scratch, prev_out, "rwkv7_branch")


def _mix_kernel(x_ref, ya_ref, yb_ref, yc_ref, gt_ref, wbr_ref, wout_ref, g1_ref, gn_ref, sh_ref, sc_ref,
                wr_ref, br_ref, ltri_ref, x1_ref, h2p_ref, topi_ref, topw_ref, topr_ref, cnt_ref, carry):
    d = x_ref.shape[1]

    @pl.when(pl.program_id(0) == 0)
    def _():
        carry[...] = jnp.zeros(carry.shape, F32)

    mixed = None
    for i, y_ref in enumerate((ya_ref, yb_ref, yc_ref)):
        term = _sigmoid(gt_ref[:, i * d:(i + 1) * d]) * _dot(_bf(y_ref[...]), wbr_ref[i])
        mixed = term if mixed is None else mixed + term
    x1 = x_ref[...] + g1_ref[...] * _dot(_bf(mixed), wout_ref[...])
    x1_ref[...] = x1
    h2 = _rms(x1, gn_ref[...]) * (1.0 + sc_ref[...]) + sh_ref[...]

    bits = pltpu.bitcast(_bf(h2).astype(F32), jnp.uint32)
    half = d // 2
    h2p_ref[...] = (bits[:, :half] >> 16) | (bits[:, half:] & jnp.uint32(0xFFFF0000))

    logits = _dot3(h2, wr_ref[...]) + br_ref[...]
    lane = lax.broadcasted_iota(jnp.int32, logits.shape, 1).astype(F32)
    vals, idxs = [], []
    for _ in range(TOP_K):
        mx = jnp.max(logits, axis=-1, keepdims=True)
        ix = jnp.min(jnp.where(logits == mx, lane, float(V7X_LANES)), axis=-1, keepdims=True)
        vals.append(mx)
        idxs.append(ix)
        logits = jnp.where(lane == ix, NEG_BIG, logits)
    exps = [jnp.exp(vv - vals[0]) for vv in vals]
    tot = exps[0]
    for e in exps[1:]:
        tot = tot + e
    onehots = [jnp.where(lane == ix, 1.0, 0.0) for ix in idxs]
    onehot = onehots[0]
    for oh in onehots[1:]:
        onehot = onehot + oh
    before = carry[0:1, :] + _dot(ltri_ref[...], _bf(onehot))
    ti = jnp.zeros(lane.shape, F32)
    tw = jnp.zeros(lane.shape, F32)
    tr = jnp.zeros(lane.shape, F32)
    for k in range(TOP_K):
        ti = jnp.where(lane == float(k), idxs[k], ti)
        tw = jnp.where(lane == float(k), exps[k] / tot, tw)
        tr = jnp.where(lane == float(k), jnp.sum(onehots[k] * before, axis=-1, keepdims=True), tr)
    topi_ref[...] = ti.astype(jnp.int32)
    topw_ref[...] = tw
    topr_ref[...] = tr.astype(jnp.int32)
    carry[...] = carry[...] + jnp.sum(onehot, axis=0, keepdims=True)
    cnt_ref[...] = carry[...]


def _mix(x, ya, yb, yc, gt, wbr_bf, wout_bf, modt, g_norm2, wr_pad, br_pad, layer, n_prompt_tiles):
    n, d = x.shape
    w = ya.shape[1]
    row = lambda c: pl.BlockSpec((ROW_TILE, c), lambda i: (i, 0))
    in_specs = [row(d), row(w), row(w), row(w), row(3 * d),
                pl.BlockSpec((None, 3, w, d), lambda i: (layer, 0, 0, 0)),
                pl.BlockSpec((None, d, d), lambda i: (layer, 0, 0)),
                _mod_spec(layer, 2, d, n_prompt_tiles),
                pl.BlockSpec((None, 1, d), lambda i: (layer, 0, 0)),
                _mod_spec(layer, 3, d, n_prompt_tiles), _mod_spec(layer, 4, d, n_prompt_tiles),
                pl.BlockSpec((None, d, V7X_LANES), lambda i: (layer, 0, 0)),
                pl.BlockSpec((None, 1, V7X_LANES), lambda i: (layer, 0, 0)),
                pl.BlockSpec((ROW_TILE, ROW_TILE), lambda i: (0, 0))]
    out_shape = [jax.ShapeDtypeStruct((n, d), F32), jax.ShapeDtypeStruct((n, d // 2), jnp.uint32),
                 jax.ShapeDtypeStruct((n, V7X_LANES), jnp.int32), jax.ShapeDtypeStruct((n, V7X_LANES), F32),
                 jax.ShapeDtypeStruct((n, V7X_LANES), jnp.int32),
                 jax.ShapeDtypeStruct((V7X_SUBLANES, V7X_LANES), F32)]
    out_specs = [row(d), row(d // 2), row(V7X_LANES), row(V7X_LANES), row(V7X_LANES),
                 pl.BlockSpec((V7X_SUBLANES, V7X_LANES), lambda i: (0, 0))]
    tile_idx = jnp.arange(ROW_TILE)
    ltri = _bf((tile_idx[None, :] < tile_idx[:, None]).astype(F32))
    return pl.pallas_call(
        _mix_kernel, out_shape=out_shape, grid=(n // ROW_TILE,), in_specs=in_specs, out_specs=out_specs,
        scratch_shapes=[pltpu.VMEM((V7X_SUBLANES, V7X_LANES), F32)],
        compiler_params=_params(("arbitrary",)), name="mix_norm_router",
    )(x, ya, yb, yc, gt, wbr_bf, wout_bf, modt, g_norm2, modt, modt, wr_pad, br_pad, ltri)


def _moe_kernel(be_ref, nact_ref, nv_ref, tok_ref, dest_ref, h2p_ref, wgu_ref, bgu_ref, wd_ref, bd_ref,
                yt_ref, xbuf, ybuf, sem):
    del be_ref
    j = pl.program_id(0)
    n_blocks = pl.num_programs(0)
    n_act = nact_ref[0]
    slot = j % 2
    d_ff = wd_ref.shape[0]

    def row_copy(s, r, dst_row):
        return pltpu.make_async_copy(ybuf.at[s, pl.ds(r, 1), :], yt_ref.at[pl.ds(dst_row, 1), :], sem.at[s])

    def start_block(s, n_rows):
        @pl.when(n_rows == MOE_BLOCK)
        def _():
            for r in range(MOE_BLOCK):
                row_copy(s, r, dest_ref[0, r]).start()

        @pl.when(n_rows < MOE_BLOCK)
        def _():
            def body(r, carry):
                row_copy(s, r, dest_ref[0, r]).start()
                return carry
            lax.fori_loop(0, n_rows, body, 0)

    def wait_block(s, n_rows):
        @pl.when(n_rows == MOE_BLOCK)
        def _():
            for r in range(MOE_BLOCK):
                row_copy(s, 0, 0).wait()

        @pl.when(n_rows < MOE_BLOCK)
        def _():
            def body(r, carry):
                row_copy(s, 0, 0).wait()
                return carry
            lax.fori_loop(0, n_rows, body, 0)

    @pl.when(jnp.logical_and(j >= 2, j - 2 < n_act))
    def _():
        wait_block(slot, nv_ref[jnp.maximum(j - 2, 0)])

    @pl.when(j < n_act)
    def _():
        for r in range(MOE_BLOCK):
            xbuf[pl.ds(r, 1), :] = h2p_ref[pl.ds(tok_ref[0, r], 1), :]
        words = xbuf[...]
        lo = pltpu.bitcast(words << 16, F32)
        hi = pltpu.bitcast(words & jnp.uint32(0xFFFF0000), F32)
        x = _bf(jnp.concatenate([lo, hi], axis=1))
        gu = _dot(x, wgu_ref[...]) + bgu_ref[...]
        gate = jnp.minimum(gu[:, :d_ff], SWIGLU_LIMIT)
        up = jnp.clip(gu[:, d_ff:], -SWIGLU_LIMIT, SWIGLU_LIMIT)
        act = (up + 1.0) * gate * _sigmoid(SWIGLU_ALPHA * gate)
        ybuf[slot] = _dot(_bf(act), wd_ref[...]) + bd_ref[...]
        start_block(slot, nv_ref[j])

    @pl.when(j == n_blocks - 1)
    def _():
        @pl.when(jnp.logical_and(j >= 1, j - 1 < n_act))
        def _():
            wait_block(1 - slot, nv_ref[jnp.maximum(j - 1, 0)])

        @pl.when(j < n_act)
        def _():
            wait_block(slot, nv_ref[j])


def _moe_routing(topi, topr, cnt, n):
    a_total = n * TOP_K
    counts = cnt[0, :N_EXPERTS].astype(jnp.int32)
    padded = (counts + MOE_BLOCK - 1) // MOE_BLOCK * MOE_BLOCK
    seg_end = jnp.cumsum(padded)
    seg_start = seg_end - padded
    n_blocks = (a_total + N_EXPERTS * (MOE_BLOCK - 1) + MOE_BLOCK - 1) // MOE_BLOCK
    n_act = (seg_end[-1] // MOE_BLOCK).astype(jnp.int32)
    blk_start = jnp.arange(n_blocks, dtype=jnp.int32) * MOE_BLOCK
    blk_e = jnp.sum(blk_start[:, None] >= seg_end[None, :], axis=1).astype(jnp.int32)
    last_e = jnp.sum((n_act - 1) * MOE_BLOCK >= seg_end).astype(jnp.int32)
    blk_e = jnp.minimum(blk_e, last_e)
    n_valid = jnp.clip(counts[blk_e] - (blk_start - seg_start[blk_e]), 0, MOE_BLOCK)
    n_valid = jnp.where(blk_start < seg_end[-1], n_valid, 0).astype(jnp.int32)
    pos = (seg_start[topi[:, :TOP_K]] + topr[:, :TOP_K]).reshape(a_total)
    tok_id = jnp.broadcast_to(jnp.arange(n, dtype=jnp.int32)[:, None], (n, TOP_K))
    choice = jnp.broadcast_to(jnp.arange(TOP_K, dtype=jnp.int32)[None, :], (n, TOP_K))
    n_pos = n_blocks * MOE_BLOCK
    tok = jnp.zeros((n_pos,), jnp.int32).at[pos].set(tok_id.reshape(a_total), unique_indices=True)
    dest = jnp.zeros((n_pos,), jnp.int32).at[pos].set((choice * n + tok_id).reshape(a_total), unique_indices=True)
    return (blk_e, n_act.reshape(1), n_valid, tok.reshape(n_blocks, 1, MOE_BLOCK),
            dest.reshape(n_blocks, 1, MOE_BLOCK), n_blocks)


def _moe(h2p, topi, topr, cnt, wgu_bf, bgu, wd_bf, bd, layer):
    n, half = h2p.shape
    d = 2 * half
    d_ff = wd_bf.shape[2]
    blk_e, n_act, n_valid, tok, dest, n_blocks = _moe_routing(topi, topr, cnt, n)
    smem_blk = lambda: pl.BlockSpec((None, 1, MOE_BLOCK), lambda j, *_: (j, 0, 0), memory_space=pltpu.SMEM)
    grid_spec = pltpu.PrefetchScalarGridSpec(
        num_scalar_prefetch=3, grid=(n_blocks,),
        in_specs=[smem_blk(), smem_blk(),
                  pl.BlockSpec((n, half), lambda j, *_: (0, 0), pipeline_mode=pl.Buffered(1)),
                  pl.BlockSpec((None, None, d, 2 * d_ff), lambda j, be, *_: (layer, be[j], 0, 0)),
                  pl.BlockSpec((None, None, 1, 2 * d_ff), lambda j, be, *_: (layer, be[j], 0, 0)),
                  pl.BlockSpec((None, None, d_ff, d), lambda j, be, *_: (layer, be[j], 0, 0)),
                  pl.BlockSpec((None, None, 1, d), lambda j, be, *_: (layer, be[j], 0, 0))],
        out_specs=pl.BlockSpec(memory_space=pl.ANY),
        scratch_shapes=[pltpu.VMEM((MOE_BLOCK, half), jnp.uint32), pltpu.VMEM((2, MOE_BLOCK, d), F32),
                        pltpu.SemaphoreType.DMA((2,))])
    return pl.pallas_call(
        _moe_kernel, out_shape=jax.ShapeDtypeStruct((n * TOP_K, d), F32), grid_spec=grid_spec,
        compiler_params=_params(("arbitrary",)), name="moe_experts",
    )(blk_e, n_act, n_valid, tok, dest, h2p, wgu_bf, bgu, wd_bf, bd)


def _final_kernel(x1_ref, g2_ref, tw_ref, y0, y1, y2, y3, g_ref, o_ref):
    x = _combine_moe(x1_ref[...], g2_ref[...], tw_ref[...], [y0[...], y1[...], y2[...], y3[...]])
    o_ref[...] = _rms(x, g_ref[...])


def _final(x1, modt, topw, yt, g_final, layer, n_prompt_tiles):
    n, d = x1.shape
    n_tiles = n // ROW_TILE
    row = lambda c: pl.BlockSpec((ROW_TILE, c), lambda i: (i, 0))
    in_specs = [row(d), _mod_spec(layer, 5, d, n_prompt_tiles), row(V7X_LANES)]
    args = [x1, modt, topw]
    for k in range(TOP_K):
        in_specs.append(pl.BlockSpec((ROW_TILE, d), functools.partial(lambda i, k: (k * n_tiles + i, 0), k=k)))
        args.append(yt)
    in_specs.append(pl.BlockSpec((1, d), lambda i: (0, 0)))
    args.append(g_final.reshape(1, d))
    return pl.pallas_call(
        _final_kernel, out_shape=jax.ShapeDtypeStruct((n, d), F32), grid=(n_tiles,),
        in_specs=in_specs, out_specs=row(d), compiler_params=_params(("arbitrary",)), name="final_norm",
    )(*args)


def _time_major(x):
    b, t, d = x.shape
    return x.transpose(1, 0, 2).reshape(t * b, d)


def _tile_mod(mod, n_seq):
    return jnp.tile(mod, (1, ROW_TILE // n_seq, 1))


def kernel(x_prompt, x_sample, state_s5_re, state_s5_im, state_hgrn, state_rwkv, state_rwkv_shift, c_prompt, c_sample, g_norm1, g_norm2, w_ada, b_ada, w_in, s5_lam_re, s5_lam_im, s5_log_step, s5_b_re, s5_b_im, s5_c_re, s5_c_im, s5_d, s5_w_glu, hg_lb, hg_gnorm, rw_mu, rw_w0, rw_w2, rw_a0, rw_a2, rw_g2, rw_k_k, rw_k_a, rw_r_k, rw_ln_w, rw_ln_b, w_branch, w_out, w_router, b_router, w_gate_up, b_gate_up, w_down, b_down, g_final):
    bp, tp, d = x_prompt.shape
    bs, ts, _ = x_sample.shape
    n_layers = w_in.shape[0]
    width = d // 2
    n_p, n_s = bp * tp, bs * ts
    n = n_p + n_s
    assert n_p % ROW_TILE == 0 and n_s % ROW_TILE == 0 and ROW_TILE % bp == 0 and ROW_TILE % bs == 0
    n_prompt_tiles = n_p // ROW_TILE
    assert n_s // ROW_TILE <= n_prompt_tiles
    reg_p = _Region(0, tp, bp)
    reg_s = _Region(n_p, ts, bs)

    x = jnp.concatenate([_time_major(x_prompt), _time_major(x_sample)], axis=0)
    mod = _ada_all(jnp.concatenate([c_prompt, c_sample], axis=0), w_ada, b_ada)
    modt = jnp.stack([_tile_mod(mod[:, :bp], bp), _tile_mod(mod[:, bp:], bs)], axis=1)

    splits = [(0, width), (width, 5 * width), (5 * width, 5 * width + rw_mu.shape[1]),
              (5 * width + rw_mu.shape[1], w_in.shape[2])]
    w_in_bf = _bf(w_in)
    wbr_bf, wout_bf, wglu_bf = _bf(w_branch), _bf(w_out), _bf(s5_w_glu)
    wgu_bf, wd_bf = _bf(w_gate_up), _bf(w_down)
    bgu = b_gate_up.reshape(n_layers, N_EXPERTS, 1, -1)
    bd = b_down.reshape(n_layers, N_EXPERTS, 1, d)
    wr_pad = jnp.pad(w_router, ((0, 0), (0, 0), (0, V7X_LANES - N_EXPERTS)))
    br_pad = jnp.pad(b_router, ((0, 0), (0, V7X_LANES - N_EXPERTS)), constant_values=NEG_BIG)[:, None, :]
    g1 = g_norm1[:, None, :]
    g2 = g_norm2[:, None, :]
    lb_soft = jax.nn.softmax(hg_lb, axis=0)
    lb_all = jnp.cumsum(lb_soft, axis=0) - lb_soft[0:1]
    rw_consts = _rwkv_consts(width)
    pad_lo = lambda m: _bf(jnp.pad(m, ((0, RW_A_LORA), (0, 0))))
    pad_hi = lambda m: _bf(jnp.pad(m, ((RW_DECAY_LORA, 0), (0, 0))))
    n_heads_rw = width // RW_HEAD

    zeros = lambda *s: jnp.zeros(s, F32)
    outs_p = {k: [] for k in ('s5r', 's5i', 'hg', 'rw', 'sh')}
    outs_s = {k: [] for k in ('s5r', 's5i', 'hg', 'rw', 'sh')}
    n_state = (width // S5_GROUP_CH) * S5_STATE

    moe_in = None
    for l in range(n_layers):
        x, (u_a, hg, pc, gt) = _inproj(x, moe_in, modt, g1, w_in_bf, l, n_prompt_tiles, splits)

        s5w = _s5_weights(s5_lam_re[l], s5_lam_im[l], s5_log_step[l], s5_b_re[l], s5_b_im[l],
                          s5_c_re[l], s5_c_im[l])
        ya, hr_p, hi_p = _s5_branch(u_a, reg_p, zeros(bp, n_state), zeros(bp, n_state), s5w, s5_d[l],
                                    wglu_bf[l], None)
        ya, hr_s, hi_s = _s5_branch(u_a, reg_s, state_s5_re[l].reshape(bs, n_state),
                                    state_s5_im[l].reshape(bs, n_state), s5w, s5_d[l], wglu_bf[l], ya)

        n_hg = width // HG_KDIM
        yb, hg_p = _hgrn_branch(hg, reg_p, zeros(bp, n_hg, HG_KDIM, HG_KDIM), lb_all[l], hg_gnorm[l], None)
        yb, hg_s = _hgrn_branch(hg, reg_s, state_hgrn[l], lb_all[l], hg_gnorm[l], yb)

        rww = dict(mu=rw_mu[l][None], w0=rw_w0[l][None], w2=pad_lo(rw_w2[l]), a0=rw_a0[l][None],
                   a2=pad_hi(rw_a2[l]), g2=_bf(rw_g2[l]), k_k=rw_k_k[l][None], k_a=rw_k_a[l][None],
                   r_k=rw_r_k[l].reshape(1, width), ln_w=rw_ln_w[l][None], ln_b=rw_ln_b[l][None])
        cols = pc.shape[1]
        yc, rw_p, sh_p = _rwkv_branch(pc, reg_p, zeros(bp, cols), zeros(bp, n_heads_rw, RW_HEAD, RW_HEAD),
                                      rww, rw_consts, None)
        yc, rw_s, sh_s = _rwkv_branch(pc, reg_s, state_rwkv_shift[l], state_rwkv[l], rww, rw_consts, yc)

        x1, h2p, topi, topw, topr, cnt = _mix(x, ya, yb, yc, gt, wbr_bf, wout_bf, modt, g2, wr_pad, br_pad, l,
                                              n_prompt_tiles)
        yt = _moe(h2p, topi, topr, cnt, wgu_bf, bgu, wd_bf, bd, l)
        x = x1
        moe_in = (topw, yt)

        for dst, vals in ((outs_p, (hr_p, hi_p, hg_p, rw_p, sh_p)), (outs_s, (hr_s, hi_s, hg_s, rw_s, sh_s))):
            for key, val in zip(('s5r', 's5i', 'hg', 'rw', 'sh'), vals):
                dst[key].append(val)

    y = _final(x, modt, moe_in[0], moe_in[1], g_final, n_layers - 1, n_prompt_tiles)
    y_prompt = y[:n_p].reshape(tp, bp, d).transpose(1, 0, 2)
    y_sample = y[n_p:].reshape(ts, bs, d).transpose(1, 0, 2)

    def pack(o, b):
        g = width // S5_GROUP_CH
        return (jnp.stack(o['s5r']).reshape(n_layers, b, g, S5_STATE),
                jnp.stack(o['s5i']).reshape(n_layers, b, g, S5_STATE),
                jnp.stack(o['hg']),
                jnp.stack(o['rw']),
                jnp.stack(o['sh']))

    return (y_prompt, y_sample) + pack(outs_p, bp) + pack(outs_s, bs)
```

```python
import functools
import math

import jax
import jax.numpy as jnp
from jax import lax
from jax.experimental import pallas as pl
from jax.experimental.pallas import tpu as pltpu

F32 = jnp.float32
BF16 = jnp.bfloat16

V7X_SUBLANES = 8
V7X_LANES = 128
V7X_VMEM_LIMIT = 60 * 1024 * 1024

ROW_TILE = 256
CHUNK_ROWS = 512
SEQ_GROUP = V7X_SUBLANES
HG_SUB = 16
MOE_BLOCK = 256

S5_GROUP_CH = 16
S5_STATE = 64
HG_KDIM = 128
RW_HEAD = 64
RW_DECAY_LORA = 64
RW_A_LORA = 64
RW_GATE_LORA = 128
RW_LN_EPS = 64e-5
N_EXPERTS = 32
TOP_K = 4
SWIGLU_LIMIT = 7.0
SWIGLU_ALPHA = 1.702
NORM_EPS = 1e-6
NEG_BIG = -1e30


def _bf(x):
    return x.astype(BF16)


def _dot(a, b):
    return jnp.dot(a, b, preferred_element_type=F32)


def _split(x):
    hi = _bf(x)
    lo = _bf(x - hi.astype(F32))
    return hi, lo


def _dot_hilo(x, w_bf):
    hi, lo = _split(x)
    return _dot(hi, w_bf) + _dot(lo, w_bf)


def _dot3(x, w):
    xh, xl = _split(x)
    wh, wl = _split(w)
    return _dot(xh, wh) + _dot(xh, wl) + _dot(xl, wh)


def _sigmoid(x):
    return 1.0 / (1.0 + jnp.exp(-x))


def _silu(x):
    return x * _sigmoid(x)


def _softplus(x):
    return jnp.maximum(x, 0.0) + jnp.log(1.0 + jnp.exp(-jnp.abs(x)))


def _gelu_tanh(x):
    return 0.5 * x * (1.0 + jnp.tanh(math.sqrt(2.0 / math.pi) * (x + 0.044715 * (x * x * x))))


def _rms(x, g):
    return x * lax.rsqrt(jnp.mean(x * x, axis=-1, keepdims=True) + NORM_EPS) * g


def _params(sem):
    return pltpu.CompilerParams(dimension_semantics=sem, vmem_limit_bytes=V7X_VMEM_LIMIT)


def _const_spec(shape):
    nd = len(shape)
    return pl.BlockSpec(shape, lambda *_: (0,) * nd, pipeline_mode=pl.Buffered(1))


def _ada_kernel(c_ref, w_ref, b_ref, o_ref):
    c = c_ref[...]
    o_ref[...] = _dot3(_silu(c), w_ref[...]) + b_ref[...]


def _ada_all(c_all, w_ada, b_ada):
    n_layers, d, d6 = w_ada.shape
    n = c_all.shape[0]
    col = d6 // 4
    return pl.pallas_call(
        _ada_kernel,
        out_shape=jax.ShapeDtypeStruct((n_layers, n, d6), F32),
        grid=(n_layers, d6 // col),
        in_specs=[pl.BlockSpec((n, d), lambda l, j: (0, 0)),
                  pl.BlockSpec((None, d, col), lambda l, j: (l, 0, j)),
                  pl.BlockSpec((None, 1, col), lambda l, j: (l, 0, j))],
        out_specs=pl.BlockSpec((None, n, col), lambda l, j: (l, 0, j)),
        compiler_params=_params(("arbitrary", "arbitrary")),
        name="ada_mod",
    )(c_all, w_ada, b_ada.reshape(n_layers, 1, d6))


def _combine_moe(x1, g2, tw, ys):
    acc = tw[:, 0:1] * ys[0]
    for k in range(1, TOP_K):
        acc = acc + tw[:, k:k + 1] * ys[k]
    return x1 + g2 * acc


def _inproj_kernel(*refs, has_moe, splits):
    if has_moe:
        x1_ref, g2_ref, tw_ref = refs[:3]
        y_refs = refs[3:3 + TOP_K]
        refs = refs[3 + TOP_K:]
        x = _combine_moe(x1_ref[...], g2_ref[...], tw_ref[...], [r[...] for r in y_refs])
    else:
        x = refs[0][...]
        refs = refs[1:]
    sh_ref, sc_ref, g_ref, w_ref = refs[:4]
    outs = refs[4:]
    if has_moe:
        outs[0][...] = x
        outs = outs[1:]
    h = _rms(x, g_ref[...]) * (1.0 + sc_ref[...]) + sh_ref[...]
    hb = _bf(h)
    for o_ref, (lo, hi) in zip(outs, splits):
        o_ref[...] = _dot(hb, w_ref[:, lo:hi])


def _mod_spec(layer, piece, d, n_prompt_tiles):
    return pl.BlockSpec((None, None, ROW_TILE, d), lambda i: (layer, i // n_prompt_tiles, 0, piece))


def _inproj(x_in, moe_in, modt, g_norm1, w_in_bf, layer, n_prompt_tiles, splits):
    n, d = x_in.shape
    n_tiles = n // ROW_TILE
    row = lambda c: pl.BlockSpec((ROW_TILE, c), lambda i: (i, 0))
    in_specs, args = [row(d)], [x_in]
    has_moe = moe_in is not None
    if has_moe:
        topw, yt = moe_in
        in_specs += [_mod_spec(layer - 1, 5, d, n_prompt_tiles), row(V7X_LANES)]
        args += [modt, topw]
        for k in range(TOP_K):
            in_specs.append(pl.BlockSpec((ROW_TILE, d), functools.partial(lambda i, k: (k * n_tiles + i, 0), k=k)))
            args.append(yt)
    in_specs += [_mod_spec(layer, 0, d, n_prompt_tiles), _mod_spec(layer, 1, d, n_prompt_tiles),
                 pl.BlockSpec((None, 1, d), lambda i: (layer, 0, 0)),
                 pl.BlockSpec((None,) + w_in_bf.shape[1:], lambda i: (layer, 0, 0), pipeline_mode=pl.Buffered(1))]
    args += [modt, modt, g_norm1, w_in_bf]
    widths = [hi - lo for lo, hi in splits]
    out_shape = [jax.ShapeDtypeStruct((n, c), F32) for c in widths]
    out_specs = [row(c) for c in widths]
    if has_moe:
        out_shape = [jax.ShapeDtypeStruct((n, d), F32)] + out_shape
        out_specs = [row(d)] + out_specs
    outs = pl.pallas_call(
        functools.partial(_inproj_kernel, has_moe=has_moe, splits=splits),
        out_shape=out_shape, grid=(n_tiles,), in_specs=in_specs, out_specs=out_specs,
        compiler_params=_params(("arbitrary",)), name="norm_inproj",
    )(*args)
    if has_moe:
        return outs[0], outs[1:]
    return x_in, outs


class _Region:
    def __init__(self, row_start, n_steps, n_seq):
        self.row_start, self.n_steps, self.n_seq = row_start, n_steps, n_seq
        assert n_seq % SEQ_GROUP == 0
        if n_seq == SEQ_GROUP:
            self.tc = min(n_steps, CHUNK_ROWS // SEQ_GROUP)
            assert n_steps % self.tc == 0
            self.block_rows = self.tc * SEQ_GROUP
            self.n_chunks = n_steps // self.tc
        else:
            self.tc = n_steps
            self.block_rows = n_steps * n_seq
            self.n_chunks = 1
        assert row_start % self.block_rows == 0
        self.block0 = row_start // self.block_rows
        self.n_groups = n_seq // SEQ_GROUP
        self.rows = self.tc * SEQ_GROUP

    def row_spec(self, cols):
        b0 = self.block0
        return pl.BlockSpec((self.block_rows, cols), lambda g, c: (b0 + c, 0))

    def group_spec(self, shape):
        nd = len(shape)
        return pl.BlockSpec(shape, lambda g, c: (g,) + (0,) * (nd - 1))

    def grid(self):
        return (self.n_groups, self.n_chunks)


def _load_rows(ref, region):
    if region.n_seq == SEQ_GROUP:
        return ref[...]
    g = pl.program_id(0)
    parts = [ref[pl.ds(pl.multiple_of(t * region.n_seq + g * SEQ_GROUP, SEQ_GROUP), SEQ_GROUP), :]
             for t in range(region.tc)]
    return jnp.concatenate(parts, axis=0)


def _store_rows(ref, val, region):
    if region.n_seq == SEQ_GROUP:
        ref[...] = val
        return
    g = pl.program_id(0)
    for t in range(region.tc):
        ref[pl.ds(pl.multiple_of(t * region.n_seq + g * SEQ_GROUP, SEQ_GROUP), SEQ_GROUP), :] = (
            val[t * SEQ_GROUP:(t + 1) * SEQ_GROUP])


def _const2(shape):
    nd = len(shape)
    return pl.BlockSpec(shape, lambda g, c: (0,) * nd)


def _recurrent_call(kernel, region, row_inputs, group_inputs, const_inputs, out_cols, group_outs, scratch,
                    prev_out, name):
    in_specs, args = [], []
    for arr, cols in row_inputs:
        in_specs.append(region.row_spec(cols))
        args.append(arr)
    for arr, shp in group_inputs:
        in_specs.append(region.group_spec(shp))
        args.append(arr)
    for arr in const_inputs:
        in_specs.append(_const2(arr.shape))
        args.append(arr)
    n_total = row_inputs[0][0].shape[0]
    out_shape = [jax.ShapeDtypeStruct((n_total, out_cols), F32)]
    out_specs = [region.row_spec(out_cols)]
    for full_shape, shp in group_outs:
        out_shape.append(jax.ShapeDtypeStruct(full_shape, F32))
        out_specs.append(region.group_spec(shp))
    if prev_out is None:
        prev_out = jnp.zeros((n_total, out_cols), F32)
    in_specs.append(pl.BlockSpec(memory_space=pl.ANY))
    args.append(prev_out)
    aliases = {len(args) - 1: 0}
    return pl.pallas_call(
        functools.partial(kernel, region=region, has_prev=True),
        out_shape=out_shape, grid=region.grid(), in_specs=in_specs, out_specs=out_specs,
        scratch_shapes=scratch, input_output_aliases=aliases,
        compiler_params=_params(("arbitrary", "arbitrary")), name=name,
    )(*args)


def _s5_kernel(u_ref, h0r_ref, h0i_ref, ar_ref, ai_ref, wb_ref, wcr_ref, wci_ref, d_ref, wglu_ref, *rest,
               region, has_prev):
    if has_prev:
        rest = rest[1:]
    y_ref, hr_out, hi_out, bur, bui, sr, si = rest
    c = pl.program_id(1)
    n_state = sr.shape[1]

    @pl.when(c == 0)
    def _():
        sr[...] = h0r_ref[...]
        si[...] = h0i_ref[...]

    u = _load_rows(u_ref, region)
    ub = _bf(u)
    bur[...] = _dot(ub, wb_ref[:, :n_state])
    bui[...] = _dot(ub, wb_ref[:, n_state:])

    lane_blk = 4 * V7X_LANES
    for cb in range(n_state // lane_blk):
        cols = slice(cb * lane_blk, (cb + 1) * lane_blk)
        ar = jnp.broadcast_to(ar_ref[:, cols], (SEQ_GROUP, lane_blk))
        ai = jnp.broadcast_to(ai_ref[:, cols], (SEQ_GROUP, lane_blk))

        def step(t, carry, cols=cols, ar=ar, ai=ai):
            hr, hi = carry
            rows = pl.ds(pl.multiple_of(t * SEQ_GROUP, SEQ_GROUP), SEQ_GROUP)
            nr = ar * hr - ai * hi + bur[rows, cols]
            ni = ar * hi + ai * hr + bui[rows, cols]
            bur[rows, cols] = nr
            bui[rows, cols] = ni
            return nr, ni

        hr, hi = lax.fori_loop(0, region.tc, step, (sr[:, cols], si[:, cols]))
        sr[:, cols] = hr
        si[:, cols] = hi

    y = _dot(_bf(bur[...]), wcr_ref[...]) - _dot(_bf(bui[...]), wci_ref[...]) + d_ref[...] * u
    g = _gelu_tanh(y)
    _store_rows(y_ref, g * _sigmoid(_dot(_bf(g), wglu_ref[...])), region)
    hr_out[...] = sr[...]
    hi_out[...] = si[...]


def _s5_weights(lam_re, lam_im, log_step, b_re, b_im, c_re, c_im):
    g, p, ch = b_re.shape
    dt = jnp.exp(log_step)[:, None]
    mag = jnp.exp(lam_re * dt)
    ab_re = mag * jnp.cos(lam_im * dt)
    ab_im = mag * jnp.sin(lam_im * dt)
    den = lam_re * lam_re + lam_im * lam_im
    zr = ((ab_re - 1.0) * lam_re + ab_im * lam_im) / den
    zi = (ab_im * lam_re - (ab_re - 1.0) * lam_im) / den
    bb_re = zr[..., None] * b_re - zi[..., None] * b_im
    bb_im = zr[..., None] * b_im + zi[..., None] * b_re
    eye = jnp.eye(g, dtype=F32)
    blk_b = lambda m: jnp.einsum('gpc,gh->gchp', m, eye).reshape(g * ch, g * p)
    blk_c = lambda m: jnp.einsum('gcp,gh->gphc', m, eye).reshape(g * p, g * ch)
    wb = _bf(jnp.concatenate([blk_b(bb_re), blk_b(bb_im)], axis=1))
    return (ab_re.reshape(1, g * p), ab_im.reshape(1, g * p), wb, _bf(blk_c(c_re)), _bf(blk_c(c_im)))


def _s5_branch(u_all, region, h0r, h0i, wts, d_skip, w_glu_bf, prev_out):
    ar, ai, wb, wcr, wci = wts
    width = u_all.shape[1]
    n_state = ar.shape[1]
    scratch = [pltpu.VMEM((region.rows, n_state), F32), pltpu.VMEM((region.rows, n_state), F32),
               pltpu.VMEM((SEQ_GROUP, n_state), F32), pltpu.VMEM((SEQ_GROUP, n_state), F32)]
    st = (SEQ_GROUP, n_state)
    return _recurrent_call(
        _s5_kernel, region, [(u_all, width)], [(h0r, st), (h0i, st)],
        [ar, ai, wb, wcr, wci, d_skip.reshape(1, width), w_glu_bf], width,
        [((region.n_seq, n_state), st), ((region.n_seq, n_state), st)], scratch, prev_out, "s5_branch")


def _hgrn_kernel(x_ref, s0_ref, lb_ref, gw_ref, *rest, region, has_prev):
    if has_prev:
        rest = rest[1:]
    y_ref, sout_ref, st_ref, q_s, k_s, v_s, lf_s, bc_s, qt_s, kh_s, o_s, vt_s, tr_s = rest
    c = pl.program_id(1)
    width = q_s.shape[1]
    n_heads = width // HG_KDIM
    sub_rows = HG_SUB * SEQ_GROUP
    rows_valid = region.rows
    rows_pad = q_s.shape[0]
    steps_valid = min(HG_SUB, region.tc)

    @pl.when(c == 0)
    def _():
        for b in range(SEQ_GROUP):
            for h in range(n_heads):
                st_ref[b, h] = s0_ref[b, h].T

    x = _load_rows(x_ref, region)
    if rows_pad > rows_valid:
        x = jnp.concatenate([x, jnp.zeros((rows_pad - rows_valid, x.shape[1]), F32)], axis=0)
    lb = lb_ref[...]
    z = x[:, width:2 * width]
    sig = _sigmoid(z)
    lf = jnp.log(lb + (1.0 - lb) * sig)
    kk = (1.0 - lb) * (1.0 - sig)
    if rows_pad > rows_valid:
        valid = lax.broadcasted_iota(jnp.int32, (rows_pad, 1), 0) < rows_valid
        lf = jnp.where(valid, lf, 0.0)
        kk = jnp.where(valid, kk, 0.0)
    q_s[...] = _silu(x[:, :width])
    k_s[...] = kk
    v_s[...] = x[:, 2 * width:3 * width]
    lf_s[...] = lf

    row_seq = lax.broadcasted_iota(jnp.int32, (sub_rows, 1), 0) % SEQ_GROUP
    lane_seq = lax.broadcasted_iota(jnp.int32, (1, sub_rows), 1) % SEQ_GROUP

    def sub_chunk(sc, _):
        r0 = pl.multiple_of(sc * sub_rows, sub_rows)
        slab = lambda t: pl.ds(pl.multiple_of(r0 + t * SEQ_GROUP, SEQ_GROUP), SEQ_GROUP)
        bcum = jnp.zeros((SEQ_GROUP, width), F32)
        for t in range(HG_SUB):
            bcum = bcum + lf_s[slab(t), :]
            bc_s[slab(t), :] = bcum
        blast = bcum
        for t in range(HG_SUB):
            bt = bc_s[slab(t), :]
            qt_s[slab(t), :] = q_s[slab(t), :] * jnp.exp(bt)
            kh_s[slab(t), :] = k_s[slab(t), :] * jnp.exp(blast - bt)
        for t in range(steps_valid):
            qt = q_s[slab(t), :]
            bt = bc_s[slab(t), :]
            acc = [jnp.zeros((SEQ_GROUP, HG_KDIM), F32) for _ in range(n_heads)]
            for s in range(t + 1):
                p = qt * k_s[slab(s), :] * jnp.exp(bt - bc_s[slab(s), :])
                vs = v_s[slab(s), :]
                for h in range(n_heads):
                    hs = slice(h * HG_KDIM, (h + 1) * HG_KDIM)
                    acc[h] = acc[h] + jnp.sum(p[:, hs], axis=-1, keepdims=True) * vs[:, hs]
            o_s[slab(t), :] = jnp.concatenate(acc, axis=-1)
        for t in range(steps_valid, HG_SUB):
            o_s[slab(t), :] = jnp.zeros((SEQ_GROUP, width), F32)
        dl = jnp.exp(blast)
        rows = pl.ds(r0, sub_rows)
        for h in range(n_heads):
            hs = slice(h * HG_KDIM, (h + 1) * HG_KDIM)
            qh = _bf(qt_s[rows, hs])
            kh = _bf(kh_s[rows, hs])
            vt_s[...] = v_s[rows, hs].T
            vt = vt_s[...]
            o_int = jnp.zeros((sub_rows, HG_KDIM), F32)
            for b in range(SEQ_GROUP):
                st = st_ref[b, h]
                tr_s[...] = st.T
                ob = _dot(qh, _bf(tr_s[...]))
                o_int = jnp.where(row_seq == b, ob, o_int)
                upd = _dot(_bf(jnp.where(lane_seq == b, vt, 0.0)), kh)
                st_ref[b, h] = st * dl[b:b + 1, hs] + upd
            o_s[rows, hs] = o_s[rows, hs] + o_int
        return 0

    lax.fori_loop(0, rows_pad // sub_rows, sub_chunk, 0)

    o = o_s[...][:rows_valid]
    g_raw = x[:rows_valid, 3 * width:4 * width]
    outs = []
    for h in range(n_heads):
        hs = slice(h * HG_KDIM, (h + 1) * HG_KDIM)
        oh = o[:, hs]
        oh = oh * lax.rsqrt(jnp.mean(oh * oh, axis=-1, keepdims=True) + NORM_EPS) * gw_ref[...]
        outs.append(oh * _silu(g_raw[:, hs]))
    _store_rows(y_ref, jnp.concatenate(outs, axis=-1), region)

    @pl.when(c == region.n_chunks - 1)
    def _():
        for b in range(SEQ_GROUP):
            for h in range(n_heads):
                sout_ref[b, h] = st_ref[b, h].T


def _hgrn_branch(hg_all, region, s0, lb, gnorm_w, prev_out):
    width = hg_all.shape[1] // 4
    n_heads = width // HG_KDIM
    sub_rows = HG_SUB * SEQ_GROUP
    rows_pad = max(region.rows, sub_rows)
    assert rows_pad % sub_rows == 0
    st = (SEQ_GROUP, n_heads, HG_KDIM, HG_KDIM)
    scratch = ([pltpu.VMEM(st, F32)] + [pltpu.VMEM((rows_pad, width), F32) for _ in range(8)]
               + [pltpu.VMEM((sub_rows, HG_KDIM), F32), pltpu.VMEM((HG_KDIM, HG_KDIM), F32)])
    return _recurrent_call(
        _hgrn_kernel, region, [(hg_all, 4 * width)], [(s0, st)],
        [lb.reshape(1, width), gnorm_w.reshape(1, HG_KDIM)], width,
        [((region.n_seq, n_heads, HG_KDIM, HG_KDIM), st)], scratch, prev_out, "hgrn2_branch")


def _rwkv_kernel(p_ref, prev0_ref, s0_ref, mu_ref, w0_ref, w2_ref, a0_ref, a2_ref, g2_ref, kk_ref, ka_ref,
                 rk_ref, lnw_ref, lnb_ref, ones_h_ref, ones_p_ref, imask_ref, *rest, region, has_prev):
    if has_prev:
        rest = rest[1:]
    (y_ref, sout_ref, last_ref, st_ref, prev_s, r_s, w_s, k_s, v_s, a_s, b_s, yr_s,
     sb_s, la_s, lv0_s, lv1_s, ly0_s, ly1_s, sa_s, vb_s, yb_s) = rest
    c = pl.program_id(1)
    width = r_s.shape[1]
    n_pairs = width // V7X_LANES
    blk = RW_HEAD

    @pl.when(c == 0)
    def _():
        for b in range(SEQ_GROUP):
            for hp in range(n_pairs):
                s_blk = jnp.concatenate([s0_ref[b, 2 * hp], s0_ref[b, 2 * hp + 1]], axis=1)
                st_ref[pl.ds((b * n_pairs + hp) * blk, blk), :] = s_blk
                sb_s[pl.ds((b * n_pairs + hp) * blk, blk), :] = _bf(s_blk)
        prev_s[...] = prev0_ref[...]

    p = _load_rows(p_ref, region)
    rows = p.shape[0]
    p_prev = jnp.concatenate([prev_s[...], p[:rows - SEQ_GROUP]], axis=0) if rows > SEQ_GROUP else prev_s[...]
    prev_s[...] = p[rows - SEQ_GROUP:]
    m = p + (p_prev - p) * mu_ref[...]
    r = m[:, :width]
    k = m[:, width:2 * width]
    v = m[:, 2 * width:3 * width]
    dwa = m[:, 3 * width:3 * width + RW_DECAY_LORA + RW_A_LORA]
    dg = m[:, 3 * width + RW_DECAY_LORA + RW_A_LORA:]
    w_log = -_softplus(-(w0_ref[...] + _dot(_bf(jnp.tanh(dwa)), w2_ref[...]))) - 0.5
    a = _sigmoid(a0_ref[...] + _dot(_bf(dwa), a2_ref[...]))
    g = _dot(_bf(_sigmoid(dg)), g2_ref[...])
    kk = k * kk_ref[...]
    kk = kk / jnp.maximum(jnp.sqrt(_dot_hilo(kk * kk, ones_h_ref[...])), 1e-12)
    k2 = k * (1.0 + (a - 1.0) * ka_ref[...])
    r_s[...] = r
    w_s[...] = jnp.exp(-jnp.exp(w_log))
    k_s[...] = k2
    v_s[...] = v
    a_s[...] = -kk
    b_s[...] = kk * a

    imask = imask_ref[...]
    ones_p = ones_p_ref[...]
    sub_iota = lax.broadcasted_iota(jnp.int32, (SEQ_GROUP, V7X_LANES), 0)

    def row(slab, b, hp, n_rows, dtype=F32):
        piece = slab[b:b + 1, hp * V7X_LANES:(hp + 1) * V7X_LANES].astype(dtype)
        return jnp.broadcast_to(piece, (n_rows, V7X_LANES))

    blocks = [(b, hp) for b in range(SEQ_GROUP) for hp in range(n_pairs)]
    rs_of = {bh: pl.ds((bh[0] * n_pairs + bh[1]) * blk, blk) for bh in blocks}

    def slab_rows(t):
        return pl.ds(pl.multiple_of(t * SEQ_GROUP, SEQ_GROUP), SEQ_GROUP)

    def prep_value(lv_ref, t):
        v_t = v_s[slab_rows(t), :]
        for bh in blocks:
            lv_ref[rs_of[bh], :] = _bf(row(v_t, *bh, blk) * imask)

    def readout(ly_ref, out_t):
        yb_s[...] = _dot(ly_ref[...], ones_p)
        for hp in range(n_pairs):
            tile = jnp.zeros((SEQ_GROUP, V7X_LANES), F32)
            for b in range(SEQ_GROUP):
                yrow = jnp.sum(yb_s[rs_of[(b, hp)], :] * imask, axis=0, keepdims=True)
                tile = jnp.where(sub_iota == b, jnp.broadcast_to(yrow, tile.shape), tile)
            yr_s[slab_rows(out_t), hp * V7X_LANES:(hp + 1) * V7X_LANES] = tile

    def half_step(t, lv_in, lv_out, ly_in, ly_out):
        a_t = a_s[slab_rows(t), :]
        for bh in blocks:
            la_s[rs_of[bh], :] = sb_s[rs_of[bh], :] * row(a_t, *bh, blk, BF16)
        sa_s[...] = _dot(la_s[...], ones_p)
        vb_s[...] = _dot(lv_in[...], ones_p)
        readout(ly_in, t)
        prep_value(lv_out, jnp.minimum(t + 1, region.tc - 1))
        w_t, k_t, b_t, r_t = w_s[slab_rows(t), :], k_s[slab_rows(t), :], b_s[slab_rows(t), :], r_s[slab_rows(t), :]
        for bh in blocks:
            rs = rs_of[bh]
            s_new = (st_ref[rs, :] * row(w_t, *bh, blk) + sa_s[rs, :] * row(b_t, *bh, blk)
                     + vb_s[rs, :] * row(k_t, *bh, blk))
            st_ref[rs, :] = s_new
            sb = _bf(s_new)
            sb_s[rs, :] = sb
            ly_out[rs, :] = sb * row(r_t, *bh, blk, BF16)

    def step_pair(i, _):
        half_step(2 * i, lv0_s, lv1_s, ly1_s, ly0_s)
        half_step(2 * i + 1, lv1_s, lv0_s, ly0_s, ly1_s)
        return 0

    assert region.tc % 2 == 0
    prep_value(lv0_s, 0)
    ly1_s[...] = jnp.zeros(ly1_s.shape, BF16)
    lax.fori_loop(0, region.tc // 2, step_pair, 0)
    readout(ly1_s, region.tc)

    y = yr_s[pl.ds(SEQ_GROUP, region.rows), :]
    ones_h = ones_h_ref[...]
    inv = 1.0 / RW_HEAD
    mean = _dot_hilo(y, ones_h) * inv
    yc = y - mean
    var = _dot_hilo(yc * yc, ones_h) * inv
    yn = yc * lax.rsqrt(var + RW_LN_EPS) * lnw_ref[...] + lnb_ref[...]
    bonus = _dot_hilo(r * k2 * rk_ref[...], ones_h) * v
    _store_rows(y_ref, (yn + bonus) * g, region)
    last_ref[...] = prev_s[...]

    @pl.when(c == region.n_chunks - 1)
    def _():
        for b in range(SEQ_GROUP):
            for hp in range(n_pairs):
                s_blk = st_ref[pl.ds((b * n_pairs + hp) * blk, blk), :]
                sout_ref[b, 2 * hp] = s_blk[:, :RW_HEAD]
                sout_ref[b, 2 * hp + 1] = s_blk[:, RW_HEAD:]


def _rwkv_consts(width):
    lane = jnp.arange(width)
    ones_h = _bf((lane[:, None] // RW_HEAD == lane[None, :] // RW_HEAD).astype(F32))
    lane_p = jnp.arange(V7X_LANES)
    ones_p = _bf((lane_p[:, None] // RW_HEAD == lane_p[None, :] // RW_HEAD).astype(F32))
    imask = (lane_p[None, :] % RW_HEAD == jnp.arange(RW_HEAD)[:, None]).astype(F32)
    return ones_h, ones_p, imask


def _rwkv_branch(pc_all, region, prev0, s0, wts, consts, prev_out):
    cols = pc_all.shape[1]
    width = wts['w0'].shape[1]
    n_pairs = width // V7X_LANES
    n_heads = width // RW_HEAD
    st_rows = SEQ_GROUP * n_pairs * RW_HEAD
    vec = lambda: pltpu.VMEM((region.rows, width), F32)
    st_buf = lambda dtype: pltpu.VMEM((st_rows, V7X_LANES), dtype)
    scratch = ([st_buf(F32), pltpu.VMEM((SEQ_GROUP, cols), F32), vec(), vec(), vec(), vec(), vec(), vec(),
                pltpu.VMEM((region.rows + SEQ_GROUP, width), F32)]
               + [st_buf(BF16) for _ in range(6)] + [st_buf(F32) for _ in range(3)])
    const_inputs = [wts['mu'], wts['w0'], wts['w2'], wts['a0'], wts['a2'], wts['g2'], wts['k_k'], wts['k_a'],
                    wts['r_k'], wts['ln_w'], wts['ln_b'], *consts]
    st_blk = (SEQ_GROUP, n_heads, RW_HEAD, RW_HEAD)
    return _recurrent_call(
        _rwkv_kernel, region, [(pc_all, cols)],
        [(prev0, (SEQ_GROUP, cols)), (s0, st_blk)], const_inputs, width,
        [((region.n_seq, n_heads, RW_HEAD, RW_HEAD), st_blk),
         ((region.n_seq, cols), (SEQ_GROUP, cols))], scratch, prev_out, "rwkv7_branch")


def _mix_kernel(x_ref, ya_ref, yb_ref, yc_ref, gt_ref, wbr_ref, wout_ref, g1_ref, gn_ref, sh_ref, sc_ref,
                wr_ref, br_ref, ltri_ref, x1_ref, h2p_ref, topi_ref, topw_ref, topr_ref, cnt_ref, carry):
    d = x_ref.shape[1]

    @pl.when(pl.program_id(0) == 0)
    def _():
        carry[...] = jnp.zeros(carry.shape, F32)

    mixed = None
    for i, y_ref in enumerate((ya_ref, yb_ref, yc_ref)):
        term = _sigmoid(gt_ref[:, i * d:(i + 1) * d]) * _dot(_bf(y_ref[...]), wbr_ref[i])
        mixed = term if mixed is None else mixed + term
    x1 = x_ref[...] + g1_ref[...] * _dot(_bf(mixed), wout_ref[...])
    x1_ref[...] = x1
    h2 = _rms(x1, gn_ref[...]) * (1.0 + sc_ref[...]) + sh_ref[...]

    bits = pltpu.bitcast(_bf(h2).astype(F32), jnp.uint32)
    half = d // 2
    h2p_ref[...] = (bits[:, :half] >> 16) | (bits[:, half:] & jnp.uint32(0xFFFF0000))

    logits = _dot3(h2, wr_ref[...]) + br_ref[...]
    lane = lax.broadcasted_iota(jnp.int32, logits.shape, 1).astype(F32)
    vals, idxs = [], []
    for _ in range(TOP_K):
        mx = jnp.max(logits, axis=-1, keepdims=True)
        ix = jnp.min(jnp.where(logits == mx, lane, float(V7X_LANES)), axis=-1, keepdims=True)
        vals.append(mx)
        idxs.append(ix)
        logits = jnp.where(lane == ix, NEG_BIG, logits)
    exps = [jnp.exp(vv - vals[0]) for vv in vals]
    tot = exps[0]
    for e in exps[1:]:
        tot = tot + e
    onehots = [jnp.where(lane == ix, 1.0, 0.0) for ix in idxs]
    onehot = onehots[0]
    for oh in onehots[1:]:
        onehot = onehot + oh
    before = carry[0:1, :] + _dot(ltri_ref[...], _bf(onehot))
    ti = jnp.zeros(lane.shape, F32)
    tw = jnp.zeros(lane.shape, F32)
    tr = jnp.zeros(lane.shape, F32)
    for k in range(TOP_K):
        ti = jnp.where(lane == float(k), idxs[k], ti)
        tw = jnp.where(lane == float(k), exps[k] / tot, tw)
        tr = jnp.where(lane == float(k), jnp.sum(onehots[k] * before, axis=-1, keepdims=True), tr)
    topi_ref[...] = ti.astype(jnp.int32)
    topw_ref[...] = tw
    topr_ref[...] = tr.astype(jnp.int32)
    carry[...] = carry[...] + jnp.sum(onehot, axis=0, keepdims=True)
    cnt_ref[...] = carry[...]


def _mix(x, ya, yb, yc, gt, wbr_bf, wout_bf, modt, g_norm2, wr_pad, br_pad, layer, n_prompt_tiles):
    n, d = x.shape
    w = ya.shape[1]
    row = lambda c: pl.BlockSpec((ROW_TILE, c), lambda i: (i, 0))
    in_specs = [row(d), row(w), row(w), row(w), row(3 * d),
                pl.BlockSpec((None, 3, w, d), lambda i: (layer, 0, 0, 0)),
                pl.BlockSpec((None, d, d), lambda i: (layer, 0, 0)),
                _mod_spec(layer, 2, d, n_prompt_tiles),
                pl.BlockSpec((None, 1, d), lambda i: (layer, 0, 0)),
                _mod_spec(layer, 3, d, n_prompt_tiles), _mod_spec(layer, 4, d, n_prompt_tiles),
                pl.BlockSpec((None, d, V7X_LANES), lambda i: (layer, 0, 0)),
                pl.BlockSpec((None, 1, V7X_LANES), lambda i: (layer, 0, 0)),
                pl.BlockSpec((ROW_TILE, ROW_TILE), lambda i: (0, 0))]
    out_shape = [jax.ShapeDtypeStruct((n, d), F32), jax.ShapeDtypeStruct((n, d // 2), jnp.uint32),
                 jax.ShapeDtypeStruct((n, V7X_LANES), jnp.int32), jax.ShapeDtypeStruct((n, V7X_LANES), F32),
                 jax.ShapeDtypeStruct((n, V7X_LANES), jnp.int32),
                 jax.ShapeDtypeStruct((V7X_SUBLANES, V7X_LANES), F32)]
    out_specs = [row(d), row(d // 2), row(V7X_LANES), row(V7X_LANES), row(V7X_LANES),
                 pl.BlockSpec((V7X_SUBLANES, V7X_LANES), lambda i: (0, 0))]
    tile_idx = jnp.arange(ROW_TILE)
    ltri = _bf((tile_idx[None, :] < tile_idx[:, None]).astype(F32))
    return pl.pallas_call(
        _mix_kernel, out_shape=out_shape, grid=(n // ROW_TILE,), in_specs=in_specs, out_specs=out_specs,
        scratch_shapes=[pltpu.VMEM((V7X_SUBLANES, V7X_LANES), F32)],
        compiler_params=_params(("arbitrary",)), name="mix_norm_router",
    )(x, ya, yb, yc, gt, wbr_bf, wout_bf, modt, g_norm2, modt, modt, wr_pad, br_pad, ltri)


def _moe_kernel(be_ref, nact_ref, nv_ref, asg_ref, h2p_ref, wgu_ref, bgu_ref, wd_ref, bd_ref,
                yt_ref, xbuf, ybuf, sem):
    del be_ref
    j = pl.program_id(0)
    n_blocks = pl.num_programs(0)
    n_act = nact_ref[0]
    slot = j % 2
    d_ff = wd_ref.shape[0]
    n_tok = h2p_ref.shape[0]

    def row_copy(s, r, dst_row):
        return pltpu.make_async_copy(ybuf.at[s, pl.ds(r, 1), :], yt_ref.at[pl.ds(dst_row, 1), :], sem.at[s])

    def dest_row(r):
        asg = asg_ref[0, r]
        return (asg % TOP_K) * n_tok + asg // TOP_K

    def start_block(s, n_rows):
        @pl.when(n_rows == MOE_BLOCK)
        def _():
            for r in range(MOE_BLOCK):
                row_copy(s, r, dest_row(r)).start()

        @pl.when(n_rows < MOE_BLOCK)
        def _():
            def body(r, carry):
                row_copy(s, r, dest_row(r)).start()
                return carry
            lax.fori_loop(0, n_rows, body, 0)

    def wait_block(s, n_rows):
        @pl.when(n_rows == MOE_BLOCK)
        def _():
            for r in range(MOE_BLOCK):
                row_copy(s, 0, 0).wait()

        @pl.when(n_rows < MOE_BLOCK)
        def _():
            def body(r, carry):
                row_copy(s, 0, 0).wait()
                return carry
            lax.fori_loop(0, n_rows, body, 0)

    @pl.when(jnp.logical_and(j >= 2, j - 2 < n_act))
    def _():
        wait_block(slot, nv_ref[jnp.maximum(j - 2, 0)])

    @pl.when(j < n_act)
    def _():
        for r in range(MOE_BLOCK):
            xbuf[pl.ds(r, 1), :] = h2p_ref[pl.ds(asg_ref[0, r] // TOP_K, 1), :]
        words = xbuf[...]
        lo = pltpu.bitcast(words << 16, F32)
        hi = pltpu.bitcast(words & jnp.uint32(0xFFFF0000), F32)
        x = _bf(jnp.concatenate([lo, hi], axis=1))
        gu = _dot(x, wgu_ref[...]) + bgu_ref[...]
        gate = jnp.minimum(gu[:, :d_ff], SWIGLU_LIMIT)
        up = jnp.clip(gu[:, d_ff:], -SWIGLU_LIMIT, SWIGLU_LIMIT)
        act = (up + 1.0) * gate * _sigmoid(SWIGLU_ALPHA * gate)
        ybuf[slot] = _dot(_bf(act), wd_ref[...]) + bd_ref[...]
        start_block(slot, nv_ref[j])

    @pl.when(j == n_blocks - 1)
    def _():
        @pl.when(jnp.logical_and(j >= 1, j - 1 < n_act))
        def _():
            wait_block(1 - slot, nv_ref[jnp.maximum(j - 1, 0)])

        @pl.when(j < n_act)
        def _():
            wait_block(slot, nv_ref[j])


def _moe_routing(topi, topr, cnt, n):
    a_total = n * TOP_K
    counts = cnt[0, :N_EXPERTS].astype(jnp.int32)
    padded = (counts + MOE_BLOCK - 1) // MOE_BLOCK * MOE_BLOCK
    seg_end = jnp.cumsum(padded)
    seg_start = seg_end - padded
    n_blocks = (a_total + N_EXPERTS * (MOE_BLOCK - 1) + MOE_BLOCK - 1) // MOE_BLOCK
    n_act = (seg_end[-1] // MOE_BLOCK).astype(jnp.int32)
    blk_start = jnp.arange(n_blocks, dtype=jnp.int32) * MOE_BLOCK
    blk_e = jnp.sum(blk_start[:, None] >= seg_end[None, :], axis=1).astype(jnp.int32)
    last_e = jnp.sum((n_act - 1) * MOE_BLOCK >= seg_end).astype(jnp.int32)
    blk_e = jnp.minimum(blk_e, last_e)
    n_valid = jnp.clip(counts[blk_e] - (blk_start - seg_start[blk_e]), 0, MOE_BLOCK)
    n_valid = jnp.where(blk_start < seg_end[-1], n_valid, 0).astype(jnp.int32)
    pos = (seg_start[topi[:, :TOP_K]] + topr[:, :TOP_K]).reshape(a_total)
    asg = jnp.zeros((n_blocks * MOE_BLOCK,), jnp.int32).at[pos].add(jnp.arange(a_total, dtype=jnp.int32))
    return blk_e, n_act.reshape(1), n_valid, asg.reshape(n_blocks, 1, MOE_BLOCK), n_blocks


def _moe(h2p, topi, topr, cnt, wgu_bf, bgu, wd_bf, bd, layer):
    n, half = h2p.shape
    d = 2 * half
    d_ff = wd_bf.shape[2]
    blk_e, n_act, n_valid, asg, n_blocks = _moe_routing(topi, topr, cnt, n)
    grid_spec = pltpu.PrefetchScalarGridSpec(
        num_scalar_prefetch=3, grid=(n_blocks,),
        in_specs=[pl.BlockSpec((None, 1, MOE_BLOCK), lambda j, *_: (j, 0, 0), memory_space=pltpu.SMEM),
                  pl.BlockSpec((n, half), lambda j, *_: (0, 0), pipeline_mode=pl.Buffered(1)),
                  pl.BlockSpec((None, None, d, 2 * d_ff), lambda j, be, *_: (layer, be[j], 0, 0)),
                  pl.BlockSpec((None, None, 1, 2 * d_ff), lambda j, be, *_: (layer, be[j], 0, 0)),
                  pl.BlockSpec((None, None, d_ff, d), lambda j, be, *_: (layer, be[j], 0, 0)),
                  pl.BlockSpec((None, None, 1, d), lambda j, be, *_: (layer, be[j], 0, 0))],
        out_specs=pl.BlockSpec(memory_space=pl.ANY),
        scratch_shapes=[pltpu.VMEM((MOE_BLOCK, half), jnp.uint32), pltpu.VMEM((2, MOE_BLOCK, d), F32),
                        pltpu.SemaphoreType.DMA((2,))])
    return pl.pallas_call(
        _moe_kernel, out_shape=jax.ShapeDtypeStruct((n * TOP_K, d), F32), grid_spec=grid_spec,
        compiler_params=_params(("arbitrary",)), name="moe_experts",
    )(blk_e, n_act, n_valid, asg, h2p, wgu_bf, bgu, wd_bf, bd)


def _final_kernel(x1_ref, g2_ref, tw_ref, y0, y1, y2, y3, g_ref, o_ref):
    x = _combine_moe(x1_ref[...], g2_ref[...], tw_ref[...], [y0[...], y1[...], y2[...], y3[...]])
    o_ref[...] = _rms(x, g_ref[...])


def _final(x1, modt, topw, yt, g_final, layer, n_prompt_tiles):
    n, d = x1.shape
    n_tiles = n // ROW_TILE
    row = lambda c: pl.BlockSpec((ROW_TILE, c), lambda i: (i, 0))
    in_specs = [row(d), _mod_spec(layer, 5, d, n_prompt_tiles), row(V7X_LANES)]
    args = [x1, modt, topw]
    for k in range(TOP_K):
        in_specs.append(pl.BlockSpec((ROW_TILE, d), functools.partial(lambda i, k: (k * n_tiles + i, 0), k=k)))
        args.append(yt)
    in_specs.append(pl.BlockSpec((1, d), lambda i: (0, 0)))
    args.append(g_final.reshape(1, d))
    return pl.pallas_call(
        _final_kernel, out_shape=jax.ShapeDtypeStruct((n, d), F32), grid=(n_tiles,),
        in_specs=in_specs, out_specs=row(d), compiler_params=_params(("arbitrary",)), name="final_norm",
    )(*args)


def _time_major(x):
    b, t, d = x.shape
    return x.transpose(1, 0, 2).reshape(t * b, d)


def _tile_mod(mod, n_seq):
    return jnp.tile(mod, (1, ROW_TILE // n_seq, 1))


def kernel(x_prompt, x_sample, state_s5_re, state_s5_im, state_hgrn, state_rwkv, state_rwkv_shift, c_prompt, c_sample, g_norm1, g_norm2, w_ada, b_ada, w_in, s5_lam_re, s5_lam_im, s5_log_step, s5_b_re, s5_b_im, s5_c_re, s5_c_im, s5_d, s5_w_glu, hg_lb, hg_gnorm, rw_mu, rw_w0, rw_w2, rw_a0, rw_a2, rw_g2, rw_k_k, rw_k_a, rw_r_k, rw_ln_w, rw_ln_b, w_branch, w_out, w_router, b_router, w_gate_up, b_gate_up, w_down, b_down, g_final):
    bp, tp, d = x_prompt.shape
    bs, ts, _ = x_sample.shape
    n_layers = w_in.shape[0]
    width = d // 2
    n_p, n_s = bp * tp, bs * ts
    n = n_p + n_s
    assert n_p % ROW_TILE == 0 and n_s % ROW_TILE == 0 and ROW_TILE % bp == 0 and ROW_TILE % bs == 0
    n_prompt_tiles = n_p // ROW_TILE
    assert n_s // ROW_TILE <= n_prompt_tiles
    reg_p = _Region(0, tp, bp)
    reg_s = _Region(n_p, ts, bs)

    x = jnp.concatenate([_time_major(x_prompt), _time_major(x_sample)], axis=0)
    mod = _ada_all(jnp.concatenate([c_prompt, c_sample], axis=0), w_ada, b_ada)
    modt = jnp.stack([_tile_mod(mod[:, :bp], bp), _tile_mod(mod[:, bp:], bs)], axis=1)

    splits = [(0, width), (width, 5 * width), (5 * width, 5 * width + rw_mu.shape[1]),
              (5 * width + rw_mu.shape[1], w_in.shape[2])]
    w_in_bf = _bf(w_in)
    wbr_bf, wout_bf, wglu_bf = _bf(w_branch), _bf(w_out), _bf(s5_w_glu)
    wgu_bf, wd_bf = _bf(w_gate_up), _bf(w_down)
    bgu = b_gate_up.reshape(n_layers, N_EXPERTS, 1, -1)
    bd = b_down.reshape(n_layers, N_EXPERTS, 1, d)
    wr_pad = jnp.pad(w_router, ((0, 0), (0, 0), (0, V7X_LANES - N_EXPERTS)))
    br_pad = jnp.pad(b_router, ((0, 0), (0, V7X_LANES - N_EXPERTS)), constant_values=NEG_BIG)[:, None, :]
    g1 = g_norm1[:, None, :]
    g2 = g_norm2[:, None, :]
    lb_soft = jax.nn.softmax(hg_lb, axis=0)
    lb_all = jnp.cumsum(lb_soft, axis=0) - lb_soft[0:1]
    rw_consts = _rwkv_consts(width)
    pad_lo = lambda m: _bf(jnp.pad(m, ((0, RW_A_LORA), (0, 0))))
    pad_hi = lambda m: _bf(jnp.pad(m, ((RW_DECAY_LORA, 0), (0, 0))))
    n_heads_rw = width // RW_HEAD

    zeros = lambda *s: jnp.zeros(s, F32)
    outs_p = {k: [] for k in ('s5r', 's5i', 'hg', 'rw', 'sh')}
    outs_s = {k: [] for k in ('s5r', 's5i', 'hg', 'rw', 'sh')}
    n_state = (width // S5_GROUP_CH) * S5_STATE

    moe_in = None
    for l in range(n_layers):
        x, (u_a, hg, pc, gt) = _inproj(x, moe_in, modt, g1, w_in_bf, l, n_prompt_tiles, splits)

        s5w = _s5_weights(s5_lam_re[l], s5_lam_im[l], s5_log_step[l], s5_b_re[l], s5_b_im[l],
                          s5_c_re[l], s5_c_im[l])
        ya, hr_p, hi_p = _s5_branch(u_a, reg_p, zeros(bp, n_state), zeros(bp, n_state), s5w, s5_d[l],
                                    wglu_bf[l], None)
        ya, hr_s, hi_s = _s5_branch(u_a, reg_s, state_s5_re[l].reshape(bs, n_state),
                                    state_s5_im[l].reshape(bs, n_state), s5w, s5_d[l], wglu_bf[l], ya)

        n_hg = width // HG_KDIM
        yb, hg_p = _hgrn_branch(hg, reg_p, zeros(bp, n_hg, HG_KDIM, HG_KDIM), lb_all[l], hg_gnorm[l], None)
        yb, hg_s = _hgrn_branch(hg, reg_s, state_hgrn[l], lb_all[l], hg_gnorm[l], yb)

        rww = dict(mu=rw_mu[l][None], w0=rw_w0[l][None], w2=pad_lo(rw_w2[l]), a0=rw_a0[l][None],
                   a2=pad_hi(rw_a2[l]), g2=_bf(rw_g2[l]), k_k=rw_k_k[l][None], k_a=rw_k_a[l][None],
                   r_k=rw_r_k[l].reshape(1, width), ln_w=rw_ln_w[l][None], ln_b=rw_ln_b[l][None])
        cols = pc.shape[1]
        yc, rw_p, sh_p = _rwkv_branch(pc, reg_p, zeros(bp, cols), zeros(bp, n_heads_rw, RW_HEAD, RW_HEAD),
                                      rww, rw_consts, None)
        yc, rw_s, sh_s = _rwkv_branch(pc, reg_s, state_rwkv_shift[l], state_rwkv[l], rww, rw_consts, yc)

        x1, h2p, topi, topw, topr, cnt = _mix(x, ya, yb, yc, gt, wbr_bf, wout_bf, modt, g2, wr_pad, br_pad, l,
                                              n_prompt_tiles)
        yt = _moe(h2p, topi, topr, cnt, wgu_bf, bgu, wd_bf, bd, l)
        x = x1
        moe_in = (topw, yt)

        for dst, vals in ((outs_p, (hr_p, hi_p, hg_p, rw_p, sh_p)), (outs_s, (hr_s, hi_s, hg_s, rw_s, sh_s))):
            for key, val in zip(('s5r', 's5i', 'hg', 'rw', 'sh'), vals):
                dst[key].append(val)

    y = _final(x, modt, moe_in[0], moe_in[1], g_final, n_layers - 1, n_prompt_tiles)
    y_prompt = y[:n_p].reshape(tp, bp, d).transpose(1, 0, 2)
    y_sample = y[n_p:].reshape(ts, bs, d).transpose(1, 0, 2)

    def pack(o, b):
        g = width // S5_GROUP_CH
        return (jnp.stack(o['s5r']).reshape(n_layers, b, g, S5_STATE),
                jnp.stack(o['s5i']).reshape(n_layers, b, g, S5_STATE),
                jnp.stack(o['hg']),
                jnp.stack(o['rw']),
                jnp.stack(o['sh']))

    return (y_prompt, y_sample) + pack(outs_p, bp) + pack(outs_s, bs)
```
